```python
import math
import jax
import jax.numpy as jnp
from jax import lax
import numpy as np

D_MODEL = 2048
BATCH = 2
SEQ = 4096
DEPTH = 2
DEC_BATCH = 128
DEC_SEQ = 1
PAST_LEN = 16384
PAGE_SIZE = 128

N_BRANCH = 4
BRANCH_W = D_MODEL // N_BRANCH
NSA_HEADS = 8
NSA_HD = BRANCH_W // NSA_HEADS
CMP_BLOCK = 32
CMP_STRIDE = 16
SLC_BLOCK = 64
SLC_TOPN = 16
WINDOW = 512
Q_BLOCK = 128
NSA_SCALE = NSA_HD ** -0.5
LRU_W = BRANCH_W
LRU_BLOCKS = 8
LRU_BW = LRU_W // LRU_BLOCKS
CONV_W = 4
LRU_C = 8.0
GLA_HEADS = 4
GLA_DK = BRANCH_W // (2 * GLA_HEADS)
GLA_DV = BRANCH_W // GLA_HEADS
GLA_RANK = 16
GLA_TAU = 16.0
GLA_CHUNK = 64
MLA_HEADS = 4
MLA_NOPE = 128
MLA_ROPE = 64
MLA_VHD = BRANCH_W // MLA_HEADS
MLA_Q_LORA = 384
MLA_KV_LORA = 128
MLA_ROW = MLA_KV_LORA + MLA_ROPE
MLA_SCALE = (MLA_NOPE + MLA_ROPE) ** -0.5
ROPE_THETA = 10000.0
NORM_EPS = 1e-6

IN_SIZES = (
    NSA_HEADS * NSA_HD,
    6 * NSA_HD,
    3 * NSA_HEADS,
    BRANCH_W,
    LRU_W,
    LRU_W,
    GLA_HEADS * GLA_DK,
    GLA_HEADS * GLA_DK,
    GLA_HEADS * GLA_DV,
    GLA_RANK,
    BRANCH_W,
    MLA_Q_LORA,
    MLA_KV_LORA,
    MLA_ROPE,
    BRANCH_W,
    N_BRANCH * D_MODEL,
)
IN_TOTAL = sum(IN_SIZES)

kernel_name = 'hybrid_nsa_rglru_gla_mla_step'


def rms_norm(x, g):
    xf = x.astype(jnp.float32)
    y = xf * lax.rsqrt(jnp.mean(xf * xf, axis=-1, keepdims=True) + NORM_EPS)
    return (y * g.astype(jnp.float32)).astype(x.dtype)


def rope(x, pos):
    d = x.shape[-1]
    inv = ROPE_THETA ** (-jnp.arange(0, d, 2, dtype=jnp.float32) / d)
    ang = pos.astype(jnp.float32)[:, None] * inv[None, :]
    cos = jnp.cos(ang)[:, None, :]
    sin = jnp.sin(ang)[:, None, :]
    x1, x2 = jnp.split(x.astype(jnp.float32), 2, axis=-1)
    return jnp.concatenate([x1 * cos - x2 * sin, x2 * cos + x1 * sin], axis=-1).astype(x.dtype)


def masked_softmax(s, mask):
    s = jnp.where(mask, s.astype(jnp.float32), -jnp.inf)
    m = jnp.max(s, axis=-1, keepdims=True)
    e = jnp.exp(s - jnp.where(jnp.isfinite(m), m, 0.0))
    return e / jnp.maximum(jnp.sum(e, axis=-1, keepdims=True), 1e-30)


def split_in(h):
    offs = np.cumsum(IN_SIZES)[:-1].tolist()
    return jnp.split(h, offs, axis=-1)


def pad_to(a, mult):
    extra = (-a.shape[1]) % mult
    if extra == 0:
        return a
    return jnp.pad(a, [(0, 0), (0, extra)] + [(0, 0)] * (a.ndim - 2))


def to_blocks(a):
    b, t = a.shape[:2]
    return jnp.swapaxes(a.reshape(b, t // Q_BLOCK, Q_BLOCK, *a.shape[2:]), 0, 1)


def from_blocks(a):
    nb, b, qb = a.shape[:3]
    return jnp.swapaxes(a, 0, 1).reshape(b, nb * qb, *a.shape[3:])


def nsa_prepare(q, kv, pos, W, l):
    b, t = q.shape[:2]
    q = rms_norm(q.reshape(b, t, NSA_HEADS, NSA_HD), W['nsa_q_g'][l])
    kc, vc, ks, vs, kw, vw = jnp.split(kv.reshape(b, t, 6, NSA_HD), 6, axis=2)
    kg = W['nsa_k_g'][l]
    kc = rms_norm(kc, kg[0])
    ks = rope(rms_norm(ks, kg[1]), pos)
    kw = rope(rms_norm(kw, kg[2]), pos)
    q_r = rope(q, pos)
    rows = jnp.concatenate([kc, vc, ks, vs], axis=2)
    win_rows = jnp.concatenate([kw, vw], axis=2)
    return q, q_r, rows, win_rows


def nsa_compress(kv_c, w_pos):
    b, L = kv_c.shape[:2]
    sub = kv_c.reshape(b, L // CMP_STRIDE, CMP_STRIDE, 2, NSA_HD)
    first = jnp.einsum('bnjcd,cjd->bncd', sub, w_pos[:, :CMP_STRIDE])
    second = jnp.einsum('bnjcd,cjd->bncd', sub, w_pos[:, CMP_STRIDE:])
    return first[:, :-1] + second[:, 1:]


def nsa_cmp_branch(q, kv_c, w_pos, q_pos):
    b, t = q.shape[:2]
    comp = nsa_compress(kv_c, w_pos)
    nc = comp.shape[1]
    blk_end = jnp.arange(nc) * CMP_STRIDE + CMP_BLOCK - 1
    mask = blk_end[None, :] <= q_pos[:, None]
    s = jnp.einsum('bthd,bnd->bthn', q, comp[:, :, 0]) * NSA_SCALE
    p = masked_softmax(s, mask[None, :, None, :])
    o = jnp.einsum('bthn,bnd->bthd', p.astype(comp.dtype), comp[:, :, 1])
    imp = jnp.pad(jnp.sum(p, axis=2), ((0, 0), (0, 0), (1, 1)))
    sub_score = imp[..., :-1] + imp[..., 1:]
    slc_score = sub_score.reshape(b, t, -1, SLC_BLOCK // CMP_STRIDE).sum(-1)
    return o, slc_score


def nsa_select(slc_score, q_pos):
    nsb = slc_score.shape[-1]
    blk = jnp.arange(nsb)[None, :]
    cur = (q_pos // SLC_BLOCK)[:, None]
    valid = blk * SLC_BLOCK <= q_pos[:, None]
    forced = (blk == 0) | (blk == cur) | (blk == cur - 1)
    score = jnp.where(valid, jnp.where(forced, jnp.inf, slc_score), -jnp.inf)
    _, idx = lax.top_k(score, min(SLC_TOPN, nsb))
    key_pos = idx[..., None] * SLC_BLOCK + jnp.arange(SLC_BLOCK)
    return key_pos.reshape(idx.shape[0], idx.shape[1], -1)


def nsa_slc_attend(q_r, kv_sel, key_pos, q_pos):
    s = jnp.einsum('bthd,btnd->bthn', q_r, kv_sel[..., 0, :]) * NSA_SCALE
    mask = (key_pos <= q_pos[None, :, None])[:, :, None, :]
    p = masked_softmax(s, mask)
    return jnp.einsum('bthn,btnd->bthd', p.astype(kv_sel.dtype), kv_sel[..., 1, :])


def nsa_slc_prompt(q_r, slc_rows, key_pos, q_pos):
    t = q_r.shape[1]

    def block(args):
        qb, kp, qp = args
        kv = jax.vmap(lambda r, i: r[i])(slc_rows, kp)
        return nsa_slc_attend(qb, kv, kp, qp)

    out = lax.map(block, (to_blocks(q_r), to_blocks(key_pos), q_pos.reshape(t // Q_BLOCK, Q_BLOCK)))
    return from_blocks(out)


def nsa_gather_paged(cache, l, page_table, new_rows, key_pos, past_len):
    b, t, n = key_pos.shape
    pos_c = jnp.minimum(key_pos, past_len - 1)
    pidx = jnp.take_along_axis(page_table, (pos_c // PAGE_SIZE).reshape(b, t * n), axis=1).reshape(b, t, n)
    past = cache[l, pidx, pos_c % PAGE_SIZE, 2:]
    new = jax.vmap(lambda r, i: r[i])(new_rows, jnp.clip(key_pos - past_len, 0, t - 1))
    return jnp.where((key_pos < past_len)[..., None, None], past, new)


def nsa_win_prompt(q_r, win_rows):
    b, t = q_r.shape[:2]
    nb = t // Q_BLOCK
    nw = WINDOW // Q_BLOCK
    kvp = jnp.pad(win_rows, ((0, 0), (WINDOW, 0), (0, 0), (0, 0))).reshape(b, nb + nw, Q_BLOCK, 2, NSA_HD)
    band = jnp.concatenate([kvp[:, j:j + nb] for j in range(nw + 1)], axis=2)
    qb = q_r.reshape(b, nb, Q_BLOCK, NSA_HEADS, NSA_HD)
    q_pos = jnp.arange(nb)[:, None] * Q_BLOCK + jnp.arange(Q_BLOCK)[None, :]
    k_pos = (jnp.arange(nb)[:, None] - nw) * Q_BLOCK + jnp.arange((nw + 1) * Q_BLOCK)[None, :]
    kp = k_pos[:, None, :]
    qp = q_pos[:, :, None]
    mask = (kp >= 0) & (kp <= qp) & (kp > qp - WINDOW)
    s = jnp.einsum('bnqhd,bnkd->bnqhk', qb, band[..., 0, :]) * NSA_SCALE
    p = masked_softmax(s, mask[None, :, :, None, :])
    o = jnp.einsum('bnqhk,bnkd->bnqhd', p.astype(band.dtype), band[..., 1, :])
    return o.reshape(b, t, NSA_HEADS, NSA_HD)


def nsa_win_sample(q_r, win_buf, win_rows, q_pos, past_len):
    lw = win_buf.shape[1]
    t = win_rows.shape[1]
    kv = jnp.concatenate([win_buf, win_rows], axis=1)
    k_pos = past_len - lw + jnp.arange(lw + t)
    mask = (k_pos[None, :] <= q_pos[:, None]) & (k_pos[None, :] > q_pos[:, None] - WINDOW)
    s = jnp.einsum('bthd,bnd->bthn', q_r, kv[:, :, 0]) * NSA_SCALE
    p = masked_softmax(s, mask[None, :, None, :])
    o = jnp.einsum('bthn,bnd->bthd', p.astype(kv.dtype), kv[:, :, 1])
    return o, kv[:, -min(WINDOW, lw + t):]


def linear_recurrence(a, b, h0):
    b = b.at[:, 0].add(a[:, 0] * h0)

    def comb(lhs, rhs):
        al, bl = lhs
        ar, br = rhs
        return al * ar, ar * bl + br

    _, h = lax.associative_scan(comb, (a, b), axis=1)
    return h


def rglru(xb, conv_prev, h0, W, l):
    b, t, _ = xb.shape
    xc = jnp.concatenate([conv_prev.astype(xb.dtype), xb], axis=1)
    cw = W['lru_conv_w'][l]
    u = xc[:, CONV_W - 1:CONV_W - 1 + t] * cw[CONV_W - 1] + W['lru_conv_b'][l]
    for j in range(CONV_W - 1):
        u = u + xc[:, j:j + t] * cw[j]
    ub = u.reshape(b, t, LRU_BLOCKS, LRU_BW)
    r = jax.nn.sigmoid(jnp.einsum('btnc,ncd->btnd', ub, W['lru_wa'][l]).reshape(b, t, LRU_W) + W['lru_ba'][l])
    i = jax.nn.sigmoid(jnp.einsum('btnc,ncd->btnd', ub, W['lru_wx'][l]).reshape(b, t, LRU_W) + W['lru_bx'][l])
    log_a = -LRU_C * r.astype(jnp.float32) * jax.nn.softplus(-W['lru_lam'][l].astype(jnp.float32))
    a = jnp.exp(log_a)
    inp = jnp.sqrt(-jnp.expm1(2.0 * log_a)) * (i * u).astype(jnp.float32)
    h = linear_recurrence(a, inp, h0.astype(jnp.float32))
    return h.astype(xb.dtype), h[:, -1], xc[:, -(CONV_W - 1):]


def gla_chunked(q, k, v, log_a, s0):
    b, t, h, dk = q.shape
    dv = v.shape[-1]
    c = GLA_CHUNK if t % GLA_CHUNK == 0 else math.gcd(t, GLA_CHUNK)
    n = t // c

    def chunks(a):
        return jnp.moveaxis(a.reshape(b, n, c, *a.shape[2:]), 1, 0)

    causal = jnp.tril(jnp.ones((c, c), dtype=bool))[None, :, :, None, None]

    def step(S, inp):
        qc, kc, vc, gc = inp
        bc = jnp.cumsum(gc, axis=1)
        inter = jnp.einsum('bthk,bhkv->bthv', qc * jnp.exp(bc), S)
        decay = jnp.exp(jnp.where(causal, bc[:, :, None] - bc[:, None, :], -jnp.inf))
        att = jnp.einsum('bthk,bshk,btshk->bths', qc, kc, decay)
        intra = jnp.einsum('bths,bshv->bthv', att, vc)
        b_end = bc[:, -1]
        S = jnp.exp(b_end)[..., None] * S + jnp.einsum('bshk,bshv->bhkv', kc * jnp.exp(b_end[:, None] - bc), vc)
        return S, inter + intra

    s_fin, o = lax.scan(step, s0.astype(jnp.float32), (chunks(q), chunks(k), chunks(v), chunks(log_a)))
    return jnp.moveaxis(o, 0, 1).reshape(b, t, h, dv), s_fin


def mla_queries(cq, pos, W, l):
    b, t, _ = cq.shape
    cq = rms_norm(cq, W['mla_cq_g'][l])
    q = (cq @ W['mla_w_uq'][l]).reshape(b, t, MLA_HEADS, MLA_NOPE + MLA_ROPE)
    q = rms_norm(q, W['mla_q_g'][l])
    q_rope = rope(q[..., MLA_NOPE:], pos)
    q_lat = jnp.einsum('bthn,hcn->bthc', q[..., :MLA_NOPE], W['mla_w_uk'][l])
    return q_lat, q_rope


def mla_rows(ckv, kr, pos, W, l):
    c = rms_norm(ckv, W['mla_ckv_g'][l])
    kr = rope(rms_norm(kr, W['mla_kr_g'][l])[:, :, None, :], pos)[:, :, 0]
    return jnp.concatenate([c, kr], axis=-1)


def mla_attend(q_lat, q_rope, lat, mask):
    c = lat[..., :MLA_KV_LORA]
    s = (jnp.einsum('bthc,bnc->bthn', q_lat, c)
         + jnp.einsum('bthr,bnr->bthn', q_rope, lat[..., MLA_KV_LORA:])) * MLA_SCALE
    p = masked_softmax(s, mask[None, :, None, :])
    return jnp.einsum('bthn,bnc->bthc', p.astype(lat.dtype), c)


def mla_prompt(q_lat, q_rope, lat):
    t = q_lat.shape[1]
    k_pos = jnp.arange(t)

    def block(args):
        ql, qr, qp = args
        return mla_attend(ql, qr, lat, k_pos[None, :] <= qp[:, None])

    out = lax.map(block, (to_blocks(q_lat), to_blocks(q_rope), jnp.arange(t).reshape(t // Q_BLOCK, Q_BLOCK)))
    return from_blocks(out)


def mixer_layer(x, l, W, C):
    b, t, _ = x.shape
    prompt = C is None
    past_len = 0 if prompt else C['page_table'].shape[1] * PAGE_SIZE
    pos = past_len + jnp.arange(t)
    h = rms_norm(x, W['norm_g'][l]) @ W['w_in'][l]
    (nsa_q, nsa_kv, nsa_g, nsa_z, lru_x, lru_z, gla_q, gla_k, gla_v, gla_a, gla_z,
     mla_cq, mla_ckv, mla_kr, mla_z, merge_logits) = split_in(h)

    q, q_r, nsa_rows, win_rows = nsa_prepare(nsa_q, nsa_kv, pos, W, l)
    if prompt:
        kv_c = nsa_rows[:, :, :2]
    else:
        past_c = C['cache_nsa_kv'][l, C['page_table'], :, :2].reshape(b, past_len, 2, NSA_HD)
        kv_c = jnp.concatenate([past_c, nsa_rows[:, :, :2]], axis=1)
    o_cmp, slc_score = nsa_cmp_branch(q, pad_to(kv_c, SLC_BLOCK), W['nsa_cmp_w'][l], pos)
    key_pos = nsa_select(slc_score, pos)
    if prompt:
        o_slc = nsa_slc_prompt(q_r, nsa_rows[:, :, 2:], key_pos, pos)
        o_win = nsa_win_prompt(q_r, win_rows)
        new_win = win_rows[:, -min(WINDOW, t):]
    else:
        kv_sel = nsa_gather_paged(C['cache_nsa_kv'], l, C['page_table'], nsa_rows[:, :, 2:], key_pos, past_len)
        o_slc = nsa_slc_attend(q_r, kv_sel, key_pos, pos)
        o_win, new_win = nsa_win_sample(q_r, C['cache_nsa_win'][l], win_rows, pos, past_len)
    g = jax.nn.sigmoid(nsa_g.reshape(b, t, NSA_HEADS, 3))
    o_nsa = (g[..., 0:1] * o_cmp + g[..., 1:2] * o_slc + g[..., 2:3] * o_win).reshape(b, t, BRANCH_W)

    if prompt:
        conv_prev = jnp.zeros((b, CONV_W - 1, LRU_W), x.dtype)
        h0 = jnp.zeros((b, LRU_W), jnp.float32)
    else:
        conv_prev = C['state_lru_conv'][l]
        h0 = C['state_lru_h'][l]
    o_lru, h_last, conv_new = rglru(lru_x, conv_prev, h0, W, l)

    qg = gla_q.reshape(b, t, GLA_HEADS, GLA_DK) * GLA_DK ** -0.5
    kg = gla_k.reshape(b, t, GLA_HEADS, GLA_DK)
    vg = gla_v.reshape(b, t, GLA_HEADS, GLA_DV)
    log_alpha = jax.nn.log_sigmoid((gla_a @ W['gla_w_a2'][l] + W['gla_b_a'][l]).astype(jnp.float32))
    log_alpha = log_alpha.reshape(b, t, GLA_HEADS, GLA_DK) / GLA_TAU
    s0 = jnp.zeros((b, GLA_HEADS, GLA_DK, GLA_DV), jnp.float32) if prompt else C['state_gla'][l]
    o_g, s_fin = gla_chunked(qg, kg, vg, log_alpha, s0)
    o_gla = rms_norm(o_g, W['gla_out_g'][l]).astype(x.dtype).reshape(b, t, BRANCH_W)

    q_lat, q_rope = mla_queries(mla_cq, pos, W, l)
    new_mla = mla_rows(mla_ckv, mla_kr, pos, W, l)
    if prompt:
        o_lat = mla_prompt(q_lat, q_rope, new_mla)
    else:
        past_lat = C['cache_mla'][l, C['page_table']].reshape(b, past_len, MLA_ROW)
        lat = jnp.concatenate([past_lat, new_mla], axis=1)
        k_pos = jnp.arange(past_len + t)
        o_lat = mla_attend(q_lat, q_rope, lat, k_pos[None, :] <= pos[:, None])
    o_mla = jnp.einsum('bthc,hcv->bthv', o_lat, W['mla_w_uv'][l]).reshape(b, t, BRANCH_W)

    branch = jnp.stack([o_nsa * jax.nn.silu(nsa_z), o_lru * jax.nn.silu(lru_z),
                        o_gla * jax.nn.silu(gla_z), o_mla * jax.nn.silu(mla_z)], axis=2)
    proj = jnp.einsum('btnw,nwd->btnd', branch, W['w_branch'][l])
    gate = jax.nn.sigmoid(merge_logits.reshape(b, t, N_BRANCH, D_MODEL))
    y = x + jnp.einsum('btnd,btnd->btd', gate, proj) @ W['w_out'][l]
    states = (nsa_rows, new_win, h_last.astype(x.dtype), conv_new, s_fin.astype(x.dtype), new_mla)
    return y, states


def setup_inputs(seed: int = 0) -> dict:
    key = jax.random.key(seed)
    keys = jax.random.split(key, 64)
    counter = [0]
    f32 = jnp.float32

    def nk():
        counter[0] += 1
        return keys[counter[0] - 1]

    def normal(shape, scale=1.0):
        return jax.random.normal(nk(), shape, f32) * scale

    def gain(shape):
        return 1.0 + 0.05 * normal(shape)

    n_pages = PAST_LEN // PAGE_SIZE
    n_pool = (DEC_BATCH * n_pages * 5) // 4
    win_len = min(WINDOW, PAST_LEN)
    page_table = jax.random.permutation(nk(), n_pool)[:DEC_BATCH * n_pages].reshape(DEC_BATCH, n_pages).astype(jnp.int32)
    u = jax.random.uniform(nk(), (DEPTH, LRU_W), f32, 0.9, 0.999)
    a_base = u ** (1.0 / LRU_C)
    lru_lam = jnp.log(a_base) - jnp.log1p(-a_base)
    return {
        'x_prompt': normal((BATCH, SEQ, D_MODEL)),
        'x_sample': normal((DEC_BATCH, DEC_SEQ, D_MODEL)),
        'cache_nsa_kv': normal((DEPTH, n_pool, PAGE_SIZE, 4, NSA_HD)),
        'cache_nsa_win': normal((DEPTH, DEC_BATCH, win_len, 2, NSA_HD)),
        'state_lru_h': normal((DEPTH, DEC_BATCH, LRU_W), 0.5),
        'state_lru_conv': normal((DEPTH, DEC_BATCH, CONV_W - 1, LRU_W)),
        'state_gla': normal((DEPTH, DEC_BATCH, GLA_HEADS, GLA_DK, GLA_DV)),
        'cache_mla': normal((DEPTH, n_pool, PAGE_SIZE, MLA_ROW)),
        'page_table': page_table,
        'norm_g': gain((DEPTH, D_MODEL)),
        'w_in': normal((DEPTH, D_MODEL, IN_TOTAL), D_MODEL ** -0.5),
        'nsa_q_g': gain((DEPTH, NSA_HD)),
        'nsa_k_g': gain((DEPTH, 3, NSA_HD)),
        'nsa_cmp_w': (1.0 + 0.2 * normal((DEPTH, 2, CMP_BLOCK, NSA_HD))) * CMP_BLOCK ** -0.5,
        'lru_conv_w': normal((DEPTH, CONV_W, LRU_W), CONV_W ** -0.5),
        'lru_conv_b': normal((DEPTH, LRU_W), 0.02),
        'lru_wa': normal((DEPTH, LRU_BLOCKS, LRU_BW, LRU_BW), LRU_BW ** -0.5),
        'lru_ba': normal((DEPTH, LRU_W), 0.02),
        'lru_wx': normal((DEPTH, LRU_BLOCKS, LRU_BW, LRU_BW), LRU_BW ** -0.5),
        'lru_bx': normal((DEPTH, LRU_W), 0.02),
        'lru_lam': lru_lam,
        'gla_w_a2': normal((DEPTH, GLA_RANK, GLA_HEADS * GLA_DK), GLA_RANK ** -0.5),
        'gla_b_a': normal((DEPTH, GLA_HEADS * GLA_DK), 0.02),
        'gla_out_g': gain((DEPTH, GLA_DV)),
        'mla_cq_g': gain((DEPTH, MLA_Q_LORA)),
        'mla_w_uq': normal((DEPTH, MLA_Q_LORA, MLA_HEADS * (MLA_NOPE + MLA_ROPE)), MLA_Q_LORA ** -0.5),
        'mla_q_g': gain((DEPTH, MLA_NOPE + MLA_ROPE)),
        'mla_ckv_g': gain((DEPTH, MLA_KV_LORA)),
        'mla_kr_g': gain((DEPTH, MLA_ROPE)),
        'mla_w_uk': normal((DEPTH, MLA_HEADS, MLA_KV_LORA, MLA_NOPE), MLA_KV_LORA ** -0.5),
        'mla_w_uv': normal((DEPTH, MLA_HEADS, MLA_KV_LORA, MLA_VHD), MLA_KV_LORA ** -0.5),
        'w_branch': normal((DEPTH, N_BRANCH, BRANCH_W, D_MODEL), BRANCH_W ** -0.5),
        'w_out': normal((DEPTH, D_MODEL, D_MODEL), 0.5 * D_MODEL ** -0.5),
    }


def reference(x_prompt, x_sample, cache_nsa_kv, cache_nsa_win, state_lru_h, state_lru_conv, state_gla,
              cache_mla, page_table, norm_g, w_in, nsa_q_g, nsa_k_g, nsa_cmp_w, lru_conv_w, lru_conv_b,
              lru_wa, lru_ba, lru_wx, lru_bx, lru_lam, gla_w_a2, gla_b_a, gla_out_g, mla_cq_g, mla_w_uq,
              mla_q_g, mla_ckv_g, mla_kr_g, mla_w_uk, mla_w_uv, w_branch, w_out):
    W = dict(norm_g=norm_g, w_in=w_in, nsa_q_g=nsa_q_g, nsa_k_g=nsa_k_g, nsa_cmp_w=nsa_cmp_w,
             lru_conv_w=lru_conv_w, lru_conv_b=lru_conv_b, lru_wa=lru_wa, lru_ba=lru_ba, lru_wx=lru_wx,
             lru_bx=lru_bx, lru_lam=lru_lam, gla_w_a2=gla_w_a2, gla_b_a=gla_b_a, gla_out_g=gla_out_g,
             mla_cq_g=mla_cq_g, mla_w_uq=mla_w_uq, mla_q_g=mla_q_g, mla_ckv_g=mla_ckv_g, mla_kr_g=mla_kr_g,
             mla_w_uk=mla_w_uk, mla_w_uv=mla_w_uv, w_branch=w_branch, w_out=w_out)
    C = dict(cache_nsa_kv=cache_nsa_kv, cache_nsa_win=cache_nsa_win, state_lru_h=state_lru_h,
             state_lru_conv=state_lru_conv, state_gla=state_gla, cache_mla=cache_mla, page_table=page_table)
    y_prompt = x_prompt
    y_sample = x_sample
    sp_list = []
    ss_list = []
    for l in range(DEPTH):
        y_prompt, sp = mixer_layer(y_prompt, l, W, None)
        y_sample, ss = mixer_layer(y_sample, l, W, C)
        sp_list.append(sp)
        ss_list.append(ss)
    nsa_kv_prompt = jnp.stack([s[0] for s in sp_list])
    nsa_kv_sample = jnp.stack([s[0] for s in ss_list])
    nsa_win_prompt = jnp.stack([s[1] for s in sp_list])
    nsa_win_sample = jnp.stack([s[1] for s in ss_list])
    lru_h_prompt = jnp.stack([s[2] for s in sp_list])
    lru_h_sample = jnp.stack([s[2] for s in ss_list])
    lru_conv_prompt = jnp.stack([s[3] for s in sp_list])
    lru_conv_sample = jnp.stack([s[3] for s in ss_list])
    gla_prompt = jnp.stack([s[4] for s in sp_list])
    gla_sample = jnp.stack([s[4] for s in ss_list])
    mla_prompt_rows = jnp.stack([s[5] for s in sp_list])
    mla_sample_rows = jnp.stack([s[5] for s in ss_list])
    return (y_prompt, y_sample, nsa_kv_prompt, nsa_kv_sample, nsa_win_prompt, nsa_win_sample,
            lru_h_prompt, lru_h_sample, lru_conv_prompt, lru_conv_sample, gla_prompt, gla_sample,
            mla_prompt_rows, mla_sample_rows)
```

```python
import functools

import numpy as np
import jax
import jax.numpy as jnp
from jax import lax
from jax.experimental import pallas as pl
from jax.experimental.pallas import tpu as pltpu

F32 = jnp.float32
BF16 = jnp.bfloat16
NEG = -1e30

N_BRANCH = 4
NSA_HEADS = 8
NSA_HD = 64
CMP_BLOCK = 32
CMP_STRIDE = 16
SLC_BLOCK = 64
SLC_TOPN = 16
WINDOW = 512
PAGE_SIZE = 128
LRU_BLOCKS = 8
CONV_W = 4
LRU_C = 8.0
GLA_HEADS = 4
GLA_DK = 64
GLA_DV = 128
GLA_RANK = 16
GLA_TAU = 16.0
GLA_CHUNK = 128
MLA_HEADS = 4
MLA_NOPE = 128
MLA_ROPE = 64
MLA_VHD = 128
MLA_Q_LORA = 384
MLA_KV_LORA = 128
MLA_ROW = MLA_KV_LORA + MLA_ROPE
ROPE_THETA = 10000.0
NORM_EPS = 1e-6

BRANCH_W = 512
NSA_SCALE = NSA_HD ** -0.5
MLA_SCALE = (MLA_NOPE + MLA_ROPE) ** -0.5

SEG_NSA = 1536
SEG_LRU = 1024
SEG_GLA = 1664
SEG_MLA = 1152

VMEM_LIMIT = 48 * 1024 * 1024


def _cparams(*sem):
    return pltpu.CompilerParams(dimension_semantics=sem, vmem_limit_bytes=VMEM_LIMIT)


def _dot(a, b):
    return jnp.dot(a.astype(BF16), b.astype(BF16), preferred_element_type=F32)


def _dot_nt(a, b):
    return lax.dot_general(a.astype(BF16), b.astype(BF16), (((1,), (1,)), ((), ())),
                           preferred_element_type=F32)


def _dot_tn(a, b):
    return lax.dot_general(a.astype(BF16), b.astype(BF16), (((0,), (0,)), ((), ())),
                           preferred_element_type=F32)


def _split3(a):
    a0 = a.astype(BF16)
    r = a - a0.astype(F32)
    a1 = r.astype(BF16)
    a2 = (r - a1.astype(F32)).astype(BF16)
    return a0, a1, a2


def _dot_x3(a, b_exact):
    a0, a1, a2 = _split3(a)
    d = functools.partial(jnp.dot, preferred_element_type=F32)
    return d(a0, b_exact) + d(a1, b_exact) + d(a2, b_exact)


def _x3_dot(a_exact, b):
    b0, b1, b2 = _split3(b)
    d = functools.partial(jnp.dot, preferred_element_type=F32)
    return d(a_exact, b0) + d(a_exact, b1) + d(a_exact, b2)


def _rope64(x, cos, sin_signed):
    w = x.shape[1]
    lane = lax.broadcasted_iota(jnp.int32, x.shape, 1)
    swapped = jnp.where((lane & 63) < 32, pltpu.roll(x, w - 32, 1), pltpu.roll(x, 32, 1))
    return x * cos + swapped * sin_signed


def _heads_to_rows(x, n_heads, hd):
    return jnp.concatenate([x[:, hd * h:hd * (h + 1)] for h in range(n_heads)], axis=0)


def _rows_to_heads(x, n_heads):
    t = x.shape[0] // n_heads
    return jnp.concatenate([x[t * h:t * (h + 1)] for h in range(n_heads)], axis=1)


def _silu(z):
    return z * jax.nn.sigmoid(z)


def _norm_kernel(x_ref, g_ref, o_ref):
    x = x_ref[...]
    ms = jnp.mean(x * x, axis=-1, keepdims=True)
    o_ref[...] = (x * lax.rsqrt(ms + NORM_EPS) * g_ref[...]).astype(o_ref.dtype)


def _rmsnorm_bf16(x, g, tm):
    n, d = x.shape
    return pl.pallas_call(
        _norm_kernel, grid=(n // tm,),
        in_specs=[pl.BlockSpec((tm, d), lambda i: (i, 0)), pl.BlockSpec((1, d), lambda i: (0, 0))],
        out_specs=pl.BlockSpec((tm, d), lambda i: (i, 0)),
        out_shape=jax.ShapeDtypeStruct((n, d), BF16),
        compiler_params=_cparams("parallel"), name="rmsnorm")(x, g.reshape(1, d))


def _mm_kernel(x_ref, w_ref, o_ref):
    o_ref[...] = jnp.dot(x_ref[...], w_ref[...], preferred_element_type=F32)


def _matmul(x, w, tm, name):
    n, k = x.shape
    c = w.shape[1]
    return pl.pallas_call(
        _mm_kernel, grid=(n // tm,),
        in_specs=[pl.BlockSpec((tm, k), lambda i: (i, 0)), pl.BlockSpec((k, c), lambda i: (0, 0))],
        out_specs=pl.BlockSpec((tm, c), lambda i: (i, 0)),
        out_shape=jax.ShapeDtypeStruct((n, c), F32),
        compiler_params=_cparams("parallel"), name=name)(x, w)


def _merge_kernel(xn_ref, b0_ref, b1_ref, b2_ref, b3_ref, wm_ref, wb_ref, o_ref, acc_ref):
    n = pl.program_id(1)

    @pl.when(n == 0)
    def _():
        acc_ref[...] = jnp.zeros_like(acc_ref)

    gate = jax.nn.sigmoid(jnp.dot(xn_ref[...], wm_ref[...], preferred_element_type=F32))
    for k, b_ref in enumerate((b0_ref, b1_ref, b2_ref, b3_ref)):
        @pl.when(n == k)
        def _(b_ref=b_ref):
            proj = jnp.dot(b_ref[...], wb_ref[0], preferred_element_type=F32)
            acc_ref[...] += gate * proj

    @pl.when(n == N_BRANCH - 1)
    def _():
        o_ref[...] = acc_ref[...].astype(o_ref.dtype)


def _merge(xn, branches, w_merge, w_branch, tm):
    n, d = xn.shape
    bw = branches[0].shape[1]
    bspec = pl.BlockSpec((tm, bw), lambda i, j: (i, 0))
    return pl.pallas_call(
        _merge_kernel, grid=(n // tm, N_BRANCH),
        in_specs=[pl.BlockSpec((tm, d), lambda i, j: (i, 0)), bspec, bspec, bspec, bspec,
                  pl.BlockSpec((d, d), lambda i, j: (0, j)),
                  pl.BlockSpec((1, bw, d), lambda i, j: (j, 0, 0))],
        out_specs=pl.BlockSpec((tm, d), lambda i, j: (i, 0)),
        out_shape=jax.ShapeDtypeStruct((n, d), BF16),
        scratch_shapes=[pltpu.VMEM((tm, d), F32)],
        compiler_params=_cparams("parallel", "arbitrary"), name="merge")(xn, *branches, w_merge, w_branch)


def _outproj_kernel(m_ref, w_ref, x_ref, o_ref):
    o_ref[...] = x_ref[...] + jnp.dot(m_ref[...], w_ref[...], preferred_element_type=F32)


def _outproj(merged, w_out, x, tm):
    n, d = x.shape
    return pl.pallas_call(
        _outproj_kernel, grid=(n // tm,),
        in_specs=[pl.BlockSpec((tm, d), lambda i: (i, 0)), pl.BlockSpec((d, d), lambda i: (0, 0)),
                  pl.BlockSpec((tm, d), lambda i: (i, 0))],
        out_specs=pl.BlockSpec((tm, d), lambda i: (i, 0)),
        out_shape=jax.ShapeDtypeStruct((n, d), F32),
        compiler_params=_cparams("parallel"), name="outproj")(merged, w_out, x)


def _nsa_prep_kernel(h_ref, cq_ref, sq_ref, ckv_ref, skv_ref, qg_ref, kvg_ref, kvm_ref, bdq_ref, bdkv_ref,
                     qn_ref, qr_ref, rows_ref, win_ref, gate_ref):
    h = h_ref[0]
    q = h[:, 0:512]
    kv = h[:, 512:896]
    ms_q = _dot_x3(q * q, bdq_ref[...]) * (1.0 / NSA_HD)
    qn = q * lax.rsqrt(ms_q + NORM_EPS) * qg_ref[...]
    qn_ref[0] = qn
    qr_ref[0] = _rope64(qn, cq_ref[...], sq_ref[...])
    ms_kv = _dot_x3(kv * kv, bdkv_ref[...]) * (1.0 / NSA_HD)
    kvn = kv * lax.rsqrt(ms_kv + NORM_EPS) * kvg_ref[...]
    kvn = jnp.where(kvm_ref[...] > 0.5, kvn, kv)
    kvr = _rope64(kvn, ckv_ref[...], skv_ref[...])
    rows_ref[0] = kvr[:, 0:256]
    win_ref[0] = kvr[:, 256:384]
    gate_ref[0] = jax.nn.sigmoid(h[:, 896:1024])


def _block_diag_ones(width, group):
    idx = np.arange(width) // group
    return jnp.asarray((idx[:, None] == idx[None, :]).astype(np.float32), dtype=BF16)


def _nsa_prep(h, tabs, q_gain, k_gain, tq):
    b, t, _ = h.shape
    cq, sq, ckv, skv = tabs
    one = jnp.ones((NSA_HD,), F32)
    kvg = jnp.concatenate([k_gain[0], one, k_gain[1], one, k_gain[2], one]).reshape(1, 384)
    kvm = jnp.concatenate([one, 0 * one, one, 0 * one, one, 0 * one]).reshape(1, 384)
    qg = jnp.tile(q_gain, NSA_HEADS).reshape(1, 512)
    tile = lambda w: pl.BlockSpec((1, tq, w), lambda bi, i: (bi, i, 0))
    tab = lambda w: pl.BlockSpec((tq, w), lambda bi, i: (i, 0))
    const = lambda r, w: pl.BlockSpec((r, w), lambda bi, i: (0, 0))
    return pl.pallas_call(
        _nsa_prep_kernel, grid=(b, t // tq),
        in_specs=[tile(SEG_NSA), tab(512), tab(512), tab(384), tab(384), const(1, 512), const(1, 384),
                  const(1, 384), const(512, 512), const(384, 384)],
        out_specs=[tile(512), tile(512), tile(256), tile(128), tile(128)],
        out_shape=[jax.ShapeDtypeStruct((b, t, 512), F32), jax.ShapeDtypeStruct((b, t, 512), F32),
                   jax.ShapeDtypeStruct((b, t, 256), F32), jax.ShapeDtypeStruct((b, t, 128), F32),
                   jax.ShapeDtypeStruct((b, t, 128), F32)],
        compiler_params=_cparams("parallel", "parallel"), name="nsa_prep",
    )(h, cq, sq, ckv, skv, qg, kvg, kvm, _block_diag_ones(512, 64), _block_diag_ones(384, 64))


def _rope_base(pos):
    inv = ROPE_THETA ** (-jnp.arange(0, 64, 2, dtype=F32) / 64)
    ang = pos.astype(F32)[:, None] * inv[None, :]
    c = jnp.cos(ang)
    s = jnp.sin(ang)
    return jnp.concatenate([c, c], axis=-1), jnp.concatenate([-s, s], axis=-1)


def _rope_tables(pos):
    c, s = _rope_base(pos)
    one = jnp.ones_like(c)
    zero = jnp.zeros_like(s)
    nsa = (jnp.tile(c, (1, NSA_HEADS)), jnp.tile(s, (1, NSA_HEADS)),
           jnp.concatenate([one, one, c, one, c, one], axis=1),
           jnp.concatenate([zero, zero, s, zero, s, zero], axis=1))
    mla = (jnp.tile(c, (1, MLA_HEADS)), jnp.tile(s, (1, MLA_HEADS)),
           jnp.concatenate([c, one], axis=1), jnp.concatenate([s, zero], axis=1))
    return nsa, mla


def _compress_kernel(rows_ref, w1_ref, w2_ref, comp_ref):
    t = rows_ref.shape[1]
    n = t // CMP_STRIDE
    first = jnp.zeros((n, 128), F32)
    second = jnp.zeros((n, 128), F32)
    for j in range(CMP_STRIDE):
        r = rows_ref[0, pl.ds(j, n, stride=CMP_STRIDE), :]
        first = first + r * w1_ref[j:j + 1, :]
        second = second + r * w2_ref[j:j + 1, :]
    nxt = pltpu.roll(second, n - 1, 0)
    row = lax.broadcasted_iota(jnp.int32, (n, 128), 0)
    comp_ref[0] = jnp.where(row < n - 1, first + nxt, 0.0)


def _cmp_weights(w_pos):
    w1 = jnp.concatenate([w_pos[0, :CMP_STRIDE], w_pos[1, :CMP_STRIDE]], axis=1)
    w2 = jnp.concatenate([w_pos[0, CMP_STRIDE:], w_pos[1, CMP_STRIDE:]], axis=1)
    return w1, w2


def _compress(rows, w_pos):
    b, t, _ = rows.shape
    w1, w2 = _cmp_weights(w_pos)
    n = t // CMP_STRIDE
    return pl.pallas_call(
        _compress_kernel, grid=(b,),
        in_specs=[pl.BlockSpec((1, t, 128), lambda bi: (bi, 0, 0)),
                  pl.BlockSpec((CMP_STRIDE, 128), lambda bi: (0, 0)),
                  pl.BlockSpec((CMP_STRIDE, 128), lambda bi: (0, 0))],
        out_specs=pl.BlockSpec((1, n, 128), lambda bi: (bi, 0, 0)),
        out_shape=jax.ShapeDtypeStruct((b, n, 128), F32),
        compiler_params=_cparams("parallel"), name="nsa_compress")(rows, w1, w2)


def _score_matrix(n_cmp, n_slc):
    m = np.zeros((n_cmp, n_slc), np.float32)
    per = SLC_BLOCK // CMP_STRIDE
    for n in range(n_cmp):
        for s in (n, n + 1):
            if s // per < n_slc:
                m[n, s // per] += 1.0
    return jnp.asarray(m, dtype=BF16)


def _nsa_cmp_kernel(qn_ref, comp_ref, m_ref, o_ref, sel_ref, *, tq):
    i = pl.program_id(1)
    qh = _heads_to_rows(qn_ref[0], NSA_HEADS, NSA_HD)
    comp = comp_ref[0]
    nc = comp.shape[0]
    s = _dot_nt(qh, comp[:, 0:64]) * NSA_SCALE
    qpos = i * tq + lax.broadcasted_iota(jnp.int32, (tq, nc), 0)
    blk_end = lax.broadcasted_iota(jnp.int32, (tq, nc), 1) * CMP_STRIDE + (CMP_BLOCK - 1)
    mask = (blk_end <= qpos)[None]
    s3 = jnp.where(mask, s.reshape(NSA_HEADS, tq, nc), NEG)
    mx = jnp.max(s3, axis=-1, keepdims=True)
    e = jnp.where(mask, jnp.exp(s3 - mx), 0.0)
    p3 = e / jnp.maximum(jnp.sum(e, axis=-1, keepdims=True), 1e-30)
    o = _dot(p3.reshape(NSA_HEADS * tq, nc), comp[:, 64:128])
    o_ref[0] = _rows_to_heads(o, NSA_HEADS)
    score = _dot_x3(jnp.sum(p3, axis=0), m_ref[...])
    nsb = score.shape[1]
    blk = lax.broadcasted_iota(jnp.int32, (tq, nsb), 1)
    qp = i * tq + lax.broadcasted_iota(jnp.int32, (tq, nsb), 0)
    cur = jnp.right_shift(qp, 6)
    valid = blk * SLC_BLOCK <= qp
    forced = (blk == 0) | (blk == cur) | (blk == cur - 1)
    sc = jnp.where(valid, jnp.where(forced, 1e30, score), -1e30)
    rank = jnp.zeros((tq, nsb), F32)
    for b in range(nsb):
        col = sc[:, b:b + 1]
        rank = rank + jnp.where((col > sc) | ((col == sc) & (blk > b)), 1.0, 0.0)
    sel_ref[0] = jnp.where((rank < SLC_TOPN) & valid, 1.0, 0.0)


def _nsa_cmp(qn, comp, tq):
    b, t, _ = qn.shape
    nc = comp.shape[1]
    nsb = t // SLC_BLOCK
    return pl.pallas_call(
        functools.partial(_nsa_cmp_kernel, tq=tq), grid=(b, t // tq),
        in_specs=[pl.BlockSpec((1, tq, 512), lambda bi, i: (bi, i, 0)),
                  pl.BlockSpec((1, nc, 128), lambda bi, i: (bi, 0, 0)),
                  pl.BlockSpec((nc, nsb), lambda bi, i: (0, 0))],
        out_specs=[pl.BlockSpec((1, tq, 512), lambda bi, i: (bi, i, 0)),
                   pl.BlockSpec((1, tq, nsb), lambda bi, i: (bi, i, 0))],
        out_shape=[jax.ShapeDtypeStruct((b, t, 512), F32), jax.ShapeDtypeStruct((b, t, nsb), F32)],
        compiler_params=_cparams("parallel", "parallel"), name="nsa_cmp")(qn, comp, _score_matrix(nc, nsb))


def _gate_expand(c):
    g = np.zeros((128, 512), np.float32)
    for h in range(NSA_HEADS):
        g[3 * h + c, NSA_HD * h:NSA_HD * (h + 1)] = 1.0
    return g


def _gate_mats():
    return jnp.asarray(np.stack([_gate_expand(c) for c in range(3)]), dtype=BF16)


def _nsa_combine(o_cmp, o_slc, o_win, gate, gx_ref, z):
    g0 = _dot_x3(gate, gx_ref[0])
    g1 = _dot_x3(gate, gx_ref[1])
    g2 = _dot_x3(gate, gx_ref[2])
    return ((g0 * o_cmp + g1 * o_slc + g2 * o_win) * _silu(z)).astype(BF16)


def _nsa_attn_kernel(qr_ref, sel_ref, rows_ref, win_ref, ocmp_ref, gate_ref, gx_ref, z_ref, out_ref,
                     m_s, l_s, acc_s, *, tq):
    i = pl.program_id(1)
    rows8 = NSA_HEADS * tq
    qh = (_heads_to_rows(qr_ref[0], NSA_HEADS, NSA_HD) * NSA_SCALE).astype(BF16)
    sel = sel_ref[0].astype(BF16)
    nsb = sel.shape[1]
    qpos = i * tq + lax.broadcasted_iota(jnp.int32, (tq, tq), 0)
    koff = lax.broadcasted_iota(jnp.int32, (tq, tq), 1)

    def reset():
        m_s[...] = jnp.full_like(m_s, NEG)
        l_s[...] = jnp.zeros_like(l_s)
        acc_s[...] = jnp.zeros_like(acc_s)

    def update(s, mask, v):
        s3 = jnp.where(mask[None], s.reshape(NSA_HEADS, tq, tq), NEG).reshape(rows8, tq)
        m_new = jnp.maximum(m_s[...], jnp.max(s3, axis=-1, keepdims=True))
        alpha = jnp.exp(m_s[...] - m_new)
        p = jnp.exp(s3 - m_new).reshape(NSA_HEADS, tq, tq)
        p = jnp.where(mask[None], p, 0.0).reshape(rows8, tq)
        l_s[...] = alpha * l_s[...] + jnp.sum(p, axis=-1, keepdims=True)
        acc_s[...] = alpha * acc_s[...] + _dot(p, v)
        m_s[...] = m_new

    def slc_tile(j, carry):
        start = pl.multiple_of(j * tq, tq)
        k = rows_ref[0, pl.ds(start, tq), 128:192]
        v = rows_ref[0, pl.ds(start, tq), 192:256]
        kpos = start + koff
        expand = (lax.broadcasted_iota(jnp.int32, (nsb, tq), 0)
                  == jnp.right_shift(start + lax.broadcasted_iota(jnp.int32, (nsb, tq), 1), 6))
        selk = jnp.dot(sel, jnp.where(expand, 1.0, 0.0).astype(BF16), preferred_element_type=F32)
        update(_dot_nt(qh, k), (kpos <= qpos) & (selk > 0.5), v)
        return carry

    reset()
    lax.fori_loop(0, i + 1, slc_tile, 0)
    o_slc = _rows_to_heads(acc_s[...] / l_s[...], NSA_HEADS)

    def win_tile(j, carry):
        start = pl.multiple_of(j * tq, tq)
        k = win_ref[0, pl.ds(start, tq), 0:64]
        v = win_ref[0, pl.ds(start, tq), 64:128]
        kpos = start + koff
        update(_dot_nt(qh, k), (kpos <= qpos) & (kpos > qpos - WINDOW), v)
        return carry

    reset()
    lax.fori_loop(jnp.maximum(i - WINDOW // tq, 0), i + 1, win_tile, 0)
    o_win = _rows_to_heads(acc_s[...] / l_s[...], NSA_HEADS)
    out_ref[0] = _nsa_combine(ocmp_ref[0], o_slc, o_win, gate_ref[0], gx_ref, z_ref[0])


def _nsa_attn(qr, sel, rows, win, o_cmp, gate, h, tq):
    b, t, _ = qr.shape
    nsb = sel.shape[2]
    tile = lambda w: pl.BlockSpec((1, tq, w), lambda bi, i: (bi, i, 0))
    return pl.pallas_call(
        functools.partial(_nsa_attn_kernel, tq=tq), grid=(b, t // tq),
        in_specs=[tile(512), tile(nsb),
                  pl.BlockSpec((1, t, 256), lambda bi, i: (bi, 0, 0)),
                  pl.BlockSpec((1, t, 128), lambda bi, i: (bi, 0, 0)),
                  tile(512), tile(128),
                  pl.BlockSpec((3, 128, 512), lambda bi, i: (0, 0, 0)),
                  pl.BlockSpec((1, tq, 512), lambda bi, i: (bi, i, 2))],
        out_specs=tile(512),
        out_shape=jax.ShapeDtypeStruct((b, t, 512), BF16),
        scratch_shapes=[pltpu.VMEM((NSA_HEADS * tq, 1), F32), pltpu.VMEM((NSA_HEADS * tq, 1), F32),
                        pltpu.VMEM((NSA_HEADS * tq, NSA_HD), F32)],
        compiler_params=_cparams("parallel", "arbitrary"), name="nsa_attn",
    )(qr, sel, rows, win, o_cmp, gate, _gate_mats(), h)


def _nsa_cmp_sample_kernel(pt_ref, *refs, n_in, n_pages, past_len):
    page_refs = refs[:n_in]
    qn_ref, w1_ref, w2_ref, m_ref, o_ref, sel_ref, fk_s, fv_s, sk_s, sv_s = refs[n_in:]
    c = pl.program_id(1)
    per_page = PAGE_SIZE // CMP_STRIDE
    for k, p_ref in enumerate(page_refs):
        fk = jnp.zeros((per_page, NSA_HD), F32)
        fv = jnp.zeros((per_page, NSA_HD), F32)
        sk = jnp.zeros((per_page, NSA_HD), F32)
        sv = jnp.zeros((per_page, NSA_HD), F32)
        for j in range(CMP_STRIDE):
            rk = p_ref[0, 0, pl.ds(j, per_page, stride=CMP_STRIDE), 0, :]
            rv = p_ref[0, 0, pl.ds(j, per_page, stride=CMP_STRIDE), 1, :]
            fk = fk + rk * w1_ref[j:j + 1, 0:64]
            fv = fv + rv * w1_ref[j:j + 1, 64:128]
            sk = sk + rk * w2_ref[j:j + 1, 0:64]
            sv = sv + rv * w2_ref[j:j + 1, 64:128]
        off = pl.multiple_of((c * n_in + k) * per_page, per_page)
        fk_s[pl.ds(off, per_page), :] = fk
        fv_s[pl.ds(off, per_page), :] = fv
        sk_s[pl.ds(off, per_page), :] = sk
        sv_s[pl.ds(off, per_page), :] = sv

    @pl.when(c == pl.num_programs(1) - 1)
    def _():
        nc = n_pages * per_page
        sk_s[pl.ds(nc, 8), :] = jnp.zeros((8, NSA_HD), F32)
        sv_s[pl.ds(nc, 8), :] = jnp.zeros((8, NSA_HD), F32)
        ck = fk_s[pl.ds(0, nc), :] + sk_s[pl.ds(1, nc), :]
        cv = fv_s[pl.ds(0, nc), :] + sv_s[pl.ds(1, nc), :]
        q = qn_ref[0]
        s = _dot_nt(q, ck) * NSA_SCALE
        blk_end = lax.broadcasted_iota(jnp.int32, (NSA_HEADS, nc), 1) * CMP_STRIDE + (CMP_BLOCK - 1)
        mask = blk_end <= past_len
        s = jnp.where(mask, s, NEG)
        e = jnp.where(mask, jnp.exp(s - jnp.max(s, axis=-1, keepdims=True)), 0.0)
        p = e / jnp.maximum(jnp.sum(e, axis=-1, keepdims=True), 1e-30)
        o_ref[0] = _dot(p, cv)
        imp = jnp.sum(p, axis=0, keepdims=True)
        score = _dot_x3(imp, m_ref[...])
        nsb = score.shape[1]
        lane = lax.broadcasted_iota(jnp.int32, (nsb, nsb), 1)
        sub = lax.broadcasted_iota(jnp.int32, (nsb, nsb), 0)
        forced_row = (lane == 0) | (lane == nsb - 1)
        sc_row = jnp.where(forced_row, 1e30, jnp.broadcast_to(score, (nsb, nsb)))
        eye = lane == sub
        sc_col = jnp.sum(jnp.where(eye, sc_row, 0.0), axis=1, keepdims=True)
        beats_row = (sc_col > sc_row) | ((sc_col == sc_row) & (sub < lane))
        rank_row = jnp.sum(jnp.where(beats_row, 1.0, 0.0), axis=0, keepdims=True)
        sel_row = rank_row < (SLC_TOPN - 1)
        beats_col = (sc_row > sc_col) | ((sc_row == sc_col) & (lane < sub))
        rank_col = jnp.sum(jnp.where(beats_col, 1.0, 0.0), axis=1, keepdims=True)
        sel_col = rank_col < (SLC_TOPN - 1)
        pos_col = jnp.sum(jnp.where(sel_row & (lane < sub), 1.0, 0.0), axis=1, keepdims=True)
        slot = lax.broadcasted_iota(jnp.int32, (nsb, 128), 1).astype(F32)
        blk_id = lax.broadcasted_iota(jnp.int32, (nsb, 128), 0).astype(F32)
        onehot = sel_col & (pos_col == slot)
        sel_ref[0] = jnp.sum(jnp.where(onehot, blk_id, 0.0), axis=0, keepdims=True).astype(jnp.int32)


def _nsa_cmp_sample(cache_kv, layer, page_table, qn, w_pos, pages_per_step):
    s, n_pages = page_table.shape
    past_len = n_pages * PAGE_SIZE
    per_page = PAGE_SIZE // CMP_STRIDE
    nc = n_pages * per_page
    nsb = past_len // SLC_BLOCK
    w1, w2 = _cmp_weights(w_pos)
    n_in = pages_per_step

    def page_spec(k):
        return pl.BlockSpec((1, 1, PAGE_SIZE, 4, NSA_HD),
                            lambda b, c, pt: (layer, pt[b, c * n_in + k], 0, 0, 0))

    const = lambda r, w: pl.BlockSpec((r, w), lambda b, c, pt: (0, 0))
    grid_spec = pltpu.PrefetchScalarGridSpec(
        num_scalar_prefetch=1, grid=(s, n_pages // n_in),
        in_specs=[page_spec(k) for k in range(n_in)]
        + [pl.BlockSpec((1, NSA_HEADS, NSA_HD), lambda b, c, pt: (b, 0, 0)),
           const(CMP_STRIDE, 128), const(CMP_STRIDE, 128), const(nc, nsb)],
        out_specs=[pl.BlockSpec((1, NSA_HEADS, NSA_HD), lambda b, c, pt: (b, 0, 0)),
                   pl.BlockSpec((1, 1, 128), lambda b, c, pt: (b, 0, 0))],
        scratch_shapes=[pltpu.VMEM((nc, NSA_HD), F32), pltpu.VMEM((nc, NSA_HD), F32),
                        pltpu.VMEM((nc + 8, NSA_HD), F32), pltpu.VMEM((nc + 8, NSA_HD), F32)])
    return pl.pallas_call(
        functools.partial(_nsa_cmp_sample_kernel, n_in=n_in, n_pages=n_pages, past_len=past_len),
        grid_spec=grid_spec,
        out_shape=[jax.ShapeDtypeStruct((s, NSA_HEADS, NSA_HD), F32),
                   jax.ShapeDtypeStruct((s, 1, 128), jnp.int32)],
        compiler_params=_cparams("parallel", "arbitrary"), name="nsa_cmp_sample",
    )(page_table, *([cache_kv] * n_in), qn, w1, w2, _score_matrix(nc, nsb))


def _softmax_rows(s, mask):
    s = jnp.where(mask, s, NEG)
    e = jnp.where(mask, jnp.exp(s - jnp.max(s, axis=-1, keepdims=True)), 0.0)
    return e / jnp.maximum(jnp.sum(e, axis=-1, keepdims=True), 1e-30)


def _nsa_slc_sample_kernel(sel_ref, pt_ref, *refs, n_sel, past_len):
    blk_refs = refs[:n_sel]
    qr_ref, new_ref, winbuf_ref, oslc_ref, owin_ref, newwin_ref = refs[n_sel:]
    q = qr_ref[0] * NSA_SCALE
    new = new_ref[0]
    ks = jnp.concatenate([r[0, 0, :, 2, :] for r in blk_refs] + [new[:, 128:192]], axis=0)
    vs = jnp.concatenate([r[0, 0, :, 3, :] for r in blk_refs] + [new[:, 192:256]], axis=0)
    s = _dot_nt(q, ks)
    p = _softmax_rows(s, jnp.full(s.shape, True))
    oslc_ref[0] = _dot(p, vs)
    lw = winbuf_ref.shape[2]
    kw = jnp.concatenate([winbuf_ref[0, 0, :, 0, :], new[:, 256:320]], axis=0)
    vw = jnp.concatenate([winbuf_ref[0, 0, :, 1, :], new[:, 320:384]], axis=0)
    s = _dot_nt(q, kw)
    k_pos = past_len - lw + lax.broadcasted_iota(jnp.int32, s.shape, 1)
    p = _softmax_rows(s, (k_pos <= past_len) & (k_pos > past_len - WINDOW))
    owin_ref[0] = _dot(p, vw)
    keep = newwin_ref.shape[2]
    drop = lw + 1 - keep
    newwin_ref[0, 0, 0:keep - 1] = winbuf_ref[0, 0, drop:lw]
    newwin_ref[0, 0, keep - 1, 0:1, :] = new[:, 256:320]
    newwin_ref[0, 0, keep - 1, 1:2, :] = new[:, 320:384]


def _nsa_slc_sample(cache_kv, cache_win, layer, page_table, sel, qr, kvr):
    s, n_pages = page_table.shape
    past_len = n_pages * PAGE_SIZE
    lw = cache_win.shape[2]
    keep = min(WINDOW, lw + 1)
    n_sel = SLC_TOPN - 1
    per_page = PAGE_SIZE // SLC_BLOCK

    def blk_spec(k):
        return pl.BlockSpec(
            (1, 1, SLC_BLOCK, 4, NSA_HD),
            lambda b, sl, pt: (layer, pt[b, sl[b, k] // per_page], sl[b, k] % per_page, 0, 0))

    grid_spec = pltpu.PrefetchScalarGridSpec(
        num_scalar_prefetch=2, grid=(s,),
        in_specs=[blk_spec(k) for k in range(n_sel)]
        + [pl.BlockSpec((1, NSA_HEADS, NSA_HD), lambda b, sl, pt: (b, 0, 0)),
           pl.BlockSpec((1, 1, 384), lambda b, sl, pt: (b, 0, 0)),
           pl.BlockSpec((1, 1, lw, 2, NSA_HD), lambda b, sl, pt: (layer, b, 0, 0, 0))],
        out_specs=[pl.BlockSpec((1, NSA_HEADS, NSA_HD), lambda b, sl, pt: (b, 0, 0)),
                   pl.BlockSpec((1, NSA_HEADS, NSA_HD), lambda b, sl, pt: (b, 0, 0)),
                   pl.BlockSpec((1, 1, keep, 2, NSA_HD), lambda b, sl, pt: (0, b, 0, 0, 0))])
    return pl.pallas_call(
        functools.partial(_nsa_slc_sample_kernel, n_sel=n_sel, past_len=past_len),
        grid_spec=grid_spec,
        out_shape=[jax.ShapeDtypeStruct((s, NSA_HEADS, NSA_HD), F32),
                   jax.ShapeDtypeStruct((s, NSA_HEADS, NSA_HD), F32),
                   jax.ShapeDtypeStruct((1, s, keep, 2, NSA_HD), F32)],
        compiler_params=_cparams("parallel"), name="nsa_slc_sample",
    )(sel, page_table, *([cache_kv] * n_sel), qr, kvr, cache_win)


def _nsa_post_sample_kernel(ocmp_ref, oslc_ref, owin_ref, gate_ref, gx_ref, z_ref, out_ref):
    out_ref[...] = _nsa_combine(ocmp_ref[...], oslc_ref[...], owin_ref[...], gate_ref[...], gx_ref, z_ref[...])


def _nsa_post_sample(o_cmp, o_slc, o_win, gate, z):
    s = gate.shape[0]
    return pl.pallas_call(
        _nsa_post_sample_kernel, out_shape=jax.ShapeDtypeStruct((s, 512), BF16), name="nsa_post_sample",
    )(o_cmp, o_slc, o_win, gate, _gate_mats(), z)


def _lru_gates(u, ig_in, wa_ref, ba_ref, wx_ref, bx_ref, lam_ref):
    del ig_in
    r = jax.nn.sigmoid(_dot(u, wa_ref[...]) + ba_ref[...])
    ig = jax.nn.sigmoid(_dot(u, wx_ref[...]) + bx_ref[...])
    log_a = -LRU_C * r * jax.nn.softplus(-lam_ref[...])
    a = jnp.exp(log_a)
    b = jnp.sqrt(jnp.tanh(-log_a) * (a * a + 1.0)) * (ig * u)
    return a, b


def _lru_kernel(h_ref, cw_ref, cb_ref, wa_ref, ba_ref, wx_ref, bx_ref, lam_ref,
                br_ref, hlast_ref, conv_ref, xbuf, hc):
    t = pl.program_id(1)
    tt = h_ref.shape[1]

    @pl.when(t == 0)
    def _():
        xbuf[0:8, :] = jnp.zeros((8, BRANCH_W), F32)
        hc[...] = jnp.zeros_like(hc)

    x = h_ref[0, :, 0:512]
    z = h_ref[0, :, 512:1024]
    xbuf[8:8 + tt, :] = x
    u = cb_ref[...] + xbuf[pl.ds(8, tt), :] * cw_ref[3:4, :]
    for j in range(CONV_W - 1):
        u = u + xbuf[pl.ds(5 + j, tt), :] * cw_ref[j:j + 1, :]
    xbuf[0:8, :] = xbuf[tt:tt + 8, :]
    a, b = _lru_gates(u, None, wa_ref, ba_ref, wx_ref, bx_ref, lam_ref)
    row = lax.broadcasted_iota(jnp.int32, (tt, BRANCH_W), 0)
    d = 1
    while d < tt:
        keep = row >= d
        b = jnp.where(keep, a * pltpu.roll(b, d, 0) + b, b)
        a = jnp.where(keep, a * pltpu.roll(a, d, 0), a)
        d *= 2
    hseq = a * hc[...] + b
    hc[...] = hseq[tt - 1:tt, :]
    br_ref[0] = (hseq * _silu(z)).astype(BF16)

    @pl.when(t == pl.num_programs(1) - 1)
    def _():
        hlast_ref[0] = hseq[tt - 1:tt, :]
        conv_ref[0] = x[tt - (CONV_W - 1):tt, :]


def _block_diag(w):
    n, c, d = w.shape
    eye = jnp.eye(n, dtype=w.dtype)
    return (w[:, :, None, :] * eye[:, None, :, None]).reshape(n * c, n * d)


def _lru_weights(p):
    row = lambda v: v.reshape(1, BRANCH_W)
    return (p["lru_conv_w"], row(p["lru_conv_b"]), _block_diag(p["lru_wa"]).astype(BF16), row(p["lru_ba"]),
            _block_diag(p["lru_wx"]).astype(BF16), row(p["lru_bx"]), row(p["lru_lam"]))


def _lru_prompt(h, p, tt):
    b, t, _ = h.shape
    const = lambda r, w: pl.BlockSpec((r, w), lambda bi, i: (0, 0))
    return pl.pallas_call(
        _lru_kernel, grid=(b, t // tt),
        in_specs=[pl.BlockSpec((1, tt, SEG_LRU), lambda bi, i: (bi, i, 0)),
                  const(CONV_W, 512), const(1, 512), const(512, 512), const(1, 512), const(512, 512),
                  const(1, 512), const(1, 512)],
        out_specs=[pl.BlockSpec((1, tt, 512), lambda bi, i: (bi, i, 0)),
                   pl.BlockSpec((1, 1, 512), lambda bi, i: (bi, 0, 0)),
                   pl.BlockSpec((1, CONV_W - 1, 512), lambda bi, i: (bi, 0, 0))],
        out_shape=[jax.ShapeDtypeStruct((b, t, 512), BF16), jax.ShapeDtypeStruct((b, 1, 512), F32),
                   jax.ShapeDtypeStruct((b, CONV_W - 1, 512), F32)],
        scratch_shapes=[pltpu.VMEM((tt + 8, 512), F32), pltpu.VMEM((1, 512), F32)],
        compiler_params=_cparams("parallel", "arbitrary"), name="lru_prompt")(h, *_lru_weights(p))


def _lru_sample_kernel(h_ref, c0_ref, c1_ref, c2_ref, h0_ref, cw_ref, cb_ref, wa_ref, ba_ref, wx_ref, bx_ref,
                       lam_ref, br_ref, hnew_ref):
    x = h_ref[:, 0:512]
    z = h_ref[:, 512:1024]
    u = cb_ref[...] + x * cw_ref[3:4, :]
    for j, c_ref in enumerate((c0_ref, c1_ref, c2_ref)):
        u = u + c_ref[...] * cw_ref[j:j + 1, :]
    a, b = _lru_gates(u, None, wa_ref, ba_ref, wx_ref, bx_ref, lam_ref)
    hnew = a * h0_ref[...] + b
    hnew_ref[...] = hnew
    br_ref[...] = (hnew * _silu(z)).astype(BF16)


def _lru_sample(h, conv_prev, h0, p):
    s = h.shape[0]
    c0, c1, c2 = (conv_prev[:, j, :] for j in range(CONV_W - 1))
    return pl.pallas_call(
        _lru_sample_kernel,
        out_shape=[jax.ShapeDtypeStruct((s, 512), BF16), jax.ShapeDtypeStruct((s, 512), F32)],
        name="lru_sample")(h, c0, c1, c2, h0, *_lru_weights(p))


def _gla_constants(c):
    tri = np.tril(np.ones((c, c), np.float32))
    rows = [tri]
    masks = []
    t = np.arange(c)
    b = c // 2
    while b >= 1:
        ref_row = (t // (2 * b)) * 2 * b + b - 1
        rows.append(tri[ref_row])
        same = (t[:, None] // (2 * b)) == (t[None, :] // (2 * b))
        masks.append((same & ((t[:, None] // b) % 2 == 1) & ((t[None, :] // b) % 2 == 0)).astype(np.float32))
        b //= 2
    return (jnp.asarray(np.concatenate(rows, axis=0), dtype=BF16), jnp.asarray(np.stack(masks), dtype=F32))


def _gla_log_alpha(a_in, wa_ref, ba_ref):
    return jax.nn.log_sigmoid(_dot(a_in, wa_ref[...]) + ba_ref[...]) * (1.0 / GLA_TAU)


def _gla_out(o, gout_ref, z):
    outs = []
    for h in range(GLA_HEADS):
        oh = o[h]
        ms = jnp.mean(oh * oh, axis=-1, keepdims=True)
        outs.append(oh * lax.rsqrt(ms + NORM_EPS) * gout_ref[...])
    return (jnp.concatenate(outs, axis=1) * _silu(z)).astype(BF16)


def _gla_kernel(h_ref, wa_ref, ba_ref, gout_ref, cum_ref, lm_ref, br_ref, state_ref, st_s):
    ci = pl.program_id(1)
    c = h_ref.shape[1]
    n_lvl = lm_ref.shape[0]

    @pl.when(ci == 0)
    def _():
        st_s[...] = jnp.zeros_like(st_s)

    h = h_ref[0]
    q = h[:, 0:256] * (GLA_DK ** -0.5)
    k = h[:, 256:512]
    v = h[:, 512:1024]
    g = _gla_log_alpha(h[:, 1024:1152], wa_ref, ba_ref)
    cums = _x3_dot(cum_ref[...], g)
    bc = cums[0:c]
    b_end = bc[c - 1:c, :]
    q_in = q * jnp.exp(bc)
    k_end = k * jnp.exp(b_end - bc)
    aq = []
    ak = []
    for lv in range(n_lvl):
        ref = cums[(lv + 1) * c:(lv + 2) * c]
        aq.append(q * jnp.exp(jnp.minimum(bc - ref, 0.0)))
        ak.append(k * jnp.exp(jnp.minimum(ref - bc, 0.0)))
    eye = (lax.broadcasted_iota(jnp.int32, (c, c), 0) == lax.broadcasted_iota(jnp.int32, (c, c), 1))
    decay_end = jnp.exp(b_end)
    outs = []
    for hd in range(GLA_HEADS):
        ks = slice(GLA_DK * hd, GLA_DK * (hd + 1))
        vh = v[:, GLA_DV * hd:GLA_DV * (hd + 1)]
        att = jnp.where(eye, _dot_nt(q[:, ks], k[:, ks]), 0.0)
        for lv in range(n_lvl):
            att = att + lm_ref[lv] * _dot_nt(aq[lv][:, ks], ak[lv][:, ks])
        st = st_s[hd]
        outs.append(_dot_nt(q_in[:, ks], st) + _dot(att, vh))
        st_s[hd] = st * decay_end[:, ks] + _dot_tn(vh, k_end[:, ks])
    br_ref[0] = _gla_out(outs, gout_ref, h[:, 1152:1664])

    @pl.when(ci == pl.num_programs(1) - 1)
    def _():
        for hd in range(GLA_HEADS):
            state_ref[0, hd] = st_s[hd].T


def _gla_weights(p):
    wa = jnp.zeros((128, GLA_HEADS * GLA_DK), F32).at[:GLA_RANK].set(p["gla_w_a2"]).astype(BF16)
    return wa, p["gla_b_a"].reshape(1, -1), p["gla_out_g"].reshape(1, -1)


def _gla_prompt(h, p):
    b, t, _ = h.shape
    c = GLA_CHUNK
    cum, lm = _gla_constants(c)
    const2 = lambda r, w: pl.BlockSpec((r, w), lambda bi, i: (0, 0))
    return pl.pallas_call(
        _gla_kernel, grid=(b, t // c),
        in_specs=[pl.BlockSpec((1, c, SEG_GLA), lambda bi, i: (bi, i, 0)),
                  const2(128, 256), const2(1, 256), const2(1, GLA_DV), const2(cum.shape[0], c),
                  pl.BlockSpec(lm.shape, lambda bi, i: (0, 0, 0))],
        out_specs=[pl.BlockSpec((1, c, 512), lambda bi, i: (bi, i, 0)),
                   pl.BlockSpec((1, GLA_HEADS, GLA_DK, GLA_DV), lambda bi, i: (bi, 0, 0, 0))],
        out_shape=[jax.ShapeDtypeStruct((b, t, 512), BF16),
                   jax.ShapeDtypeStruct((b, GLA_HEADS, GLA_DK, GLA_DV), F32)],
        scratch_shapes=[pltpu.VMEM((GLA_HEADS, GLA_DV, GLA_DK), F32)],
        compiler_params=_cparams("parallel", "arbitrary"), name="gla_prompt")(h, *_gla_weights(p), cum, lm)


def _gla_sample_kernel(h_ref, s0_ref, wa_ref, ba_ref, gout_ref, br_ref, snew_ref):
    h = h_ref[0]
    q = h[:, 0:256] * (GLA_DK ** -0.5)
    k = h[:, 256:512]
    v = h[:, 512:1024]
    g = _gla_log_alpha(h[:, 1024:1152], wa_ref, ba_ref)
    decay = jnp.exp(g)
    eye = (lax.broadcasted_iota(jnp.int32, (GLA_DK, GLA_DK), 0)
           == lax.broadcasted_iota(jnp.int32, (GLA_DK, GLA_DK), 1))
    to_col = lambda r: jnp.sum(jnp.where(eye, r, 0.0), axis=1, keepdims=True)
    outs = []
    for hd in range(GLA_HEADS):
        ks = slice(GLA_DK * hd, GLA_DK * (hd + 1))
        vh = v[:, GLA_DV * hd:GLA_DV * (hd + 1)]
        s0 = s0_ref[0, 0, hd]
        qk = jnp.sum(q[:, ks] * k[:, ks], axis=-1, keepdims=True)
        outs.append(_dot(q[:, ks] * decay[:, ks], s0) + qk * vh)
        snew_ref[0, hd] = to_col(decay[:, ks]) * s0 + to_col(k[:, ks]) * vh
    br_ref[0] = _gla_out(outs, gout_ref, h[:, 1152:1664])


def _gla_sample(h, state, layer, p):
    s = h.shape[0]
    const2 = lambda r, w: pl.BlockSpec((r, w), lambda b: (0, 0))
    br, snew = pl.pallas_call(
        _gla_sample_kernel, grid=(s,),
        in_specs=[pl.BlockSpec((1, 1, SEG_GLA), lambda b: (b, 0, 0)),
                  pl.BlockSpec((1, 1, GLA_HEADS, GLA_DK, GLA_DV), lambda b: (layer, b, 0, 0, 0)),
                  const2(128, 256), const2(1, 256), const2(1, GLA_DV)],
        out_specs=[pl.BlockSpec((1, 1, 512), lambda b: (b, 0, 0)),
                   pl.BlockSpec((1, GLA_HEADS, GLA_DK, GLA_DV), lambda b: (b, 0, 0, 0))],
        out_shape=[jax.ShapeDtypeStruct((s, 1, 512), BF16),
                   jax.ShapeDtypeStruct((s, GLA_HEADS, GLA_DK, GLA_DV), F32)],
        compiler_params=_cparams("parallel"), name="gla_sample",
    )(h.reshape(s, 1, SEG_GLA), state, *_gla_weights(p))
    return br.reshape(s, 512), snew


def _mla_prep_kernel(h_ref, cq_g_ref, wuq_ref, grp_ref, qg_ref, wuk_ref, ckv_g_ref, kr_g_ref,
                     cosq_ref, sinq_ref, cosk_ref, sink_ref, qcat_ref, lat_ref):
    h = h_ref[0]
    cq = h[:, 0:384]
    cqn = cq * lax.rsqrt(jnp.mean(cq * cq, axis=-1, keepdims=True) + NORM_EPS) * cq_g_ref[...]
    q = _dot(cqn, wuq_ref[...])
    ms = _dot_x3(q * q, grp_ref[...]) * (1.0 / (MLA_NOPE + MLA_ROPE))
    qn = q * lax.rsqrt(ms + NORM_EPS) * qg_ref[...]
    q_rope = _rope64(qn[:, 512:768], cosq_ref[...], sinq_ref[...])
    for hd in range(MLA_HEADS):
        q_lat = _dot(qn[:, MLA_NOPE * hd:MLA_NOPE * (hd + 1)], wuk_ref[hd])
        qcat_ref[0, hd] = jnp.concatenate(
            [q_lat, q_rope[:, MLA_ROPE * hd:MLA_ROPE * (hd + 1)]], axis=1).astype(BF16)
    ckv = h[:, 384:512]
    c = ckv * lax.rsqrt(jnp.mean(ckv * ckv, axis=-1, keepdims=True) + NORM_EPS) * ckv_g_ref[...]
    kr = h[:, 512:640]
    ms_kr = jnp.sum(kr * kr, axis=-1, keepdims=True) * (1.0 / MLA_ROPE)
    krn = kr * lax.rsqrt(ms_kr + NORM_EPS) * kr_g_ref[...]
    krr = _rope64(krn, cosk_ref[...], sink_ref[...])
    lat_ref[0] = jnp.concatenate([c, krr[:, 0:MLA_ROPE]], axis=1)


def _mla_weights(p):
    hw = MLA_NOPE + MLA_ROPE
    w = p["mla_w_uq"].reshape(MLA_Q_LORA, MLA_HEADS, hw)
    wuq = jnp.concatenate([w[:, :, :MLA_NOPE].reshape(MLA_Q_LORA, -1),
                           w[:, :, MLA_NOPE:].reshape(MLA_Q_LORA, -1)], axis=1).astype(BF16)
    qg = jnp.concatenate([jnp.tile(p["mla_q_g"][:MLA_NOPE], MLA_HEADS),
                          jnp.tile(p["mla_q_g"][MLA_NOPE:], MLA_HEADS)]).reshape(1, -1)
    head = np.concatenate([np.arange(MLA_HEADS * MLA_NOPE) // MLA_NOPE, np.arange(MLA_HEADS * MLA_ROPE) // MLA_ROPE])
    grp = jnp.asarray((head[:, None] == head[None, :]).astype(np.float32), dtype=BF16)
    wuk_t = jnp.swapaxes(p["mla_w_uk"], 1, 2).astype(BF16)
    kr_g = jnp.concatenate([p["mla_kr_g"], jnp.zeros((128 - MLA_ROPE,), F32)]).reshape(1, 128)
    return (p["mla_cq_g"].reshape(1, -1), wuq, grp, qg, wuk_t, p["mla_ckv_g"].reshape(1, -1), kr_g)


def _mla_prep(h, tabs, p, tq):
    b, t, _ = h.shape
    tab = lambda w: pl.BlockSpec((tq, w), lambda bi, i: (i, 0))
    const2 = lambda r, w: pl.BlockSpec((r, w), lambda bi, i: (0, 0))
    return pl.pallas_call(
        _mla_prep_kernel, grid=(b, t // tq),
        in_specs=[pl.BlockSpec((1, tq, SEG_MLA), lambda bi, i: (bi, i, 0)),
                  const2(1, 384), const2(384, 768), const2(768, 768), const2(1, 768),
                  pl.BlockSpec((MLA_HEADS, MLA_NOPE, MLA_KV_LORA), lambda bi, i: (0, 0, 0)),
                  const2(1, 128), const2(1, 128), tab(256), tab(256), tab(128), tab(128)],
        out_specs=[pl.BlockSpec((1, MLA_HEADS, tq, MLA_ROW), lambda bi, i: (bi, 0, i, 0)),
                   pl.BlockSpec((1, tq, MLA_ROW), lambda bi, i: (bi, i, 0))],
        out_shape=[jax.ShapeDtypeStruct((b, MLA_HEADS, t, MLA_ROW), BF16),
                   jax.ShapeDtypeStruct((b, t, MLA_ROW), F32)],
        compiler_params=_cparams("parallel", "parallel"), name="mla_prep")(h, *_mla_weights(p), *tabs)


def _mla_up(o_lat, wuv_ref, z):
    outs = [_dot(o_lat[hd], wuv_ref[hd]) for hd in range(MLA_HEADS)]
    return (jnp.concatenate(outs, axis=1) * _silu(z)).astype(BF16)


def _mla_attn_kernel(q_ref, lat_ref, wuv_ref, z_ref, out_ref, m_s, l_s, acc_s, *, tq):
    i = pl.program_id(1)
    rows = MLA_HEADS * tq
    qh = q_ref[0].reshape(rows, MLA_ROW)
    m_s[...] = jnp.full_like(m_s, NEG)
    l_s[...] = jnp.zeros_like(l_s)
    acc_s[...] = jnp.zeros_like(acc_s)
    qpos = i * tq + lax.broadcasted_iota(jnp.int32, (tq, tq), 0)
    koff = lax.broadcasted_iota(jnp.int32, (tq, tq), 1)

    def tile(j, carry):
        start = pl.multiple_of(j * tq, tq)
        lat = lat_ref[0, pl.ds(start, tq), :].astype(BF16)
        s = _dot_nt(qh, lat) * MLA_SCALE
        mask = ((start + koff) <= qpos)[None]
        s = jnp.where(mask, s.reshape(MLA_HEADS, tq, tq), NEG).reshape(rows, tq)
        m_new = jnp.maximum(m_s[...], jnp.max(s, axis=-1, keepdims=True))
        alpha = jnp.exp(m_s[...] - m_new)
        p = jnp.where(mask, jnp.exp(s - m_new).reshape(MLA_HEADS, tq, tq), 0.0).reshape(rows, tq)
        l_s[...] = alpha * l_s[...] + jnp.sum(p, axis=-1, keepdims=True)
        acc_s[...] = alpha * acc_s[...] + _dot(p, lat[:, 0:MLA_KV_LORA])
        m_s[...] = m_new
        return carry

    lax.fori_loop(0, i + 1, tile, 0)
    o = acc_s[...] / l_s[...]
    out_ref[0] = _mla_up([o[tq * hd:tq * (hd + 1)] for hd in range(MLA_HEADS)], wuv_ref, z_ref[0])


def _mla_attn(qcat, lat, w_uv, h, tq):
    b, _, t, _ = qcat.shape
    rows = MLA_HEADS * tq
    return pl.pallas_call(
        functools.partial(_mla_attn_kernel, tq=tq), grid=(b, t // tq),
        in_specs=[pl.BlockSpec((1, MLA_HEADS, tq, MLA_ROW), lambda bi, i: (bi, 0, i, 0)),
                  pl.BlockSpec((1, t, MLA_ROW), lambda bi, i: (bi, 0, 0)),
                  pl.BlockSpec((MLA_HEADS, MLA_KV_LORA, MLA_VHD), lambda bi, i: (0, 0, 0)),
                  pl.BlockSpec((1, tq, 512), lambda bi, i: (bi, i, 0))],
        out_specs=pl.BlockSpec((1, tq, 512), lambda bi, i: (bi, i, 0)),
        out_shape=jax.ShapeDtypeStruct((b, t, 512), BF16),
        scratch_shapes=[pltpu.VMEM((rows, 1), F32), pltpu.VMEM((rows, 1), F32),
                        pltpu.VMEM((rows, MLA_KV_LORA), F32)],
        compiler_params=_cparams("parallel", "arbitrary"), name="mla_attn")(qcat, lat, w_uv.astype(BF16), h)


def _mla_sample_kernel(pt_ref, *refs, n_in):
    page_refs = refs[:n_in]
    q_ref, new_ref, o_ref, m_s, l_s, acc_s = refs[n_in:]
    c = pl.program_id(1)

    @pl.when(c == 0)
    def _():
        m_s[...] = jnp.full_like(m_s, NEG)
        l_s[...] = jnp.zeros_like(l_s)
        acc_s[...] = jnp.zeros_like(acc_s)

    q = q_ref[0]

    keys = [r[0, 0].astype(BF16) for r in page_refs]
    s = jnp.concatenate([_dot_nt(q, kk) for kk in keys], axis=1) * MLA_SCALE
    m_new = jnp.maximum(m_s[...], jnp.max(s, axis=-1, keepdims=True))
    alpha = jnp.exp(m_s[...] - m_new)
    p = jnp.exp(s - m_new)
    l_s[...] = alpha * l_s[...] + jnp.sum(p, axis=-1, keepdims=True)
    acc = alpha * acc_s[...]
    for k, kk in enumerate(keys):
        acc = acc + _dot(p[:, PAGE_SIZE * k:PAGE_SIZE * (k + 1)], kk[:, 0:MLA_KV_LORA])
    acc_s[...] = acc
    m_s[...] = m_new

    @pl.when(c == pl.num_programs(1) - 1)
    def _():
        new = new_ref[0]
        s_new = jnp.sum(q.astype(F32) * new, axis=-1, keepdims=True) * MLA_SCALE
        m_fin = jnp.maximum(m_s[...], s_new)
        a_fin = jnp.exp(m_s[...] - m_fin)
        p_new = jnp.exp(s_new - m_fin)
        o_ref[0] = (a_fin * acc_s[...] + p_new * new[:, 0:MLA_KV_LORA]) / (a_fin * l_s[...] + p_new)


def _mla_sample_attn(cache_mla, layer, page_table, q, new_rows, pages_per_step):
    s, n_pages = page_table.shape
    n_in = pages_per_step

    def page_spec(k):
        return pl.BlockSpec((1, 1, PAGE_SIZE, MLA_ROW), lambda b, c, pt: (layer, pt[b, c * n_in + k], 0, 0))

    grid_spec = pltpu.PrefetchScalarGridSpec(
        num_scalar_prefetch=1, grid=(s, n_pages // n_in),
        in_specs=[page_spec(k) for k in range(n_in)]
        + [pl.BlockSpec((1, MLA_HEADS, MLA_ROW), lambda b, c, pt: (b, 0, 0)),
           pl.BlockSpec((1, 1, MLA_ROW), lambda b, c, pt: (b, 0, 0))],
        out_specs=pl.BlockSpec((1, MLA_HEADS, MLA_KV_LORA), lambda b, c, pt: (b, 0, 0)),
        scratch_shapes=[pltpu.VMEM((MLA_HEADS, 1), F32), pltpu.VMEM((MLA_HEADS, 1), F32),
                        pltpu.VMEM((MLA_HEADS, MLA_KV_LORA), F32)])
    return pl.pallas_call(
        functools.partial(_mla_sample_kernel, n_in=n_in), grid_spec=grid_spec,
        out_shape=jax.ShapeDtypeStruct((s, MLA_HEADS, MLA_KV_LORA), F32),
        compiler_params=_cparams("parallel", "arbitrary"), name="mla_sample_attn",
    )(page_table, *([cache_mla] * n_in), q, new_rows)


def _mla_post_sample_kernel(o_ref, wuv_ref, z_ref, out_ref):
    out_ref[...] = _mla_up([o_ref[:, hd, :] for hd in range(MLA_HEADS)], wuv_ref, z_ref[...])


def _mla_post_sample(o_lat, w_uv, z):
    s = o_lat.shape[0]
    return pl.pallas_call(
        _mla_post_sample_kernel, out_shape=jax.ShapeDtypeStruct((s, 512), BF16), name="mla_post_sample",
    )(o_lat, w_uv.astype(BF16), z)


def _pad_cols(w, width):
    return jnp.pad(w, ((0, 0), (0, width - w.shape[1])))


def _input_weights(w_in):
    sizes = (512, 384, 24, 512, 512, 512, 256, 256, 512, 16, 512, 384, 128, 64, 512)
    offs = np.concatenate([[0], np.cumsum(sizes)])
    seg = [w_in[:, offs[i]:offs[i + 1]] for i in range(len(sizes))]
    (nsa_q, nsa_kv, nsa_g, nsa_z, lru_x, lru_z, gla_q, gla_k, gla_v, gla_a, gla_z,
     mla_cq, mla_ckv, mla_kr, mla_z) = seg
    w_nsa = jnp.concatenate([nsa_q, nsa_kv, _pad_cols(nsa_g, 128), nsa_z], axis=1)
    w_lru = jnp.concatenate([lru_x, lru_z], axis=1)
    w_gla = jnp.concatenate([gla_q, gla_k, gla_v, _pad_cols(gla_a, 128), gla_z], axis=1)
    w_mla = jnp.concatenate([mla_cq, mla_ckv, _pad_cols(mla_kr, 128), mla_z], axis=1)
    w_merge = w_in[:, offs[-1]:]
    return tuple(w.astype(BF16) for w in (w_nsa, w_lru, w_gla, w_mla, w_merge))


def _layer_params(params, layer):
    return {k: v[layer] for k, v in params.items()}


def _prompt_layer(x, p, tabs):
    b, t, d = x.shape
    n = b * t
    w_nsa, w_lru, w_gla, w_mla, w_merge = _input_weights(p["w_in"])
    xn = _rmsnorm_bf16(x.reshape(n, d), p["norm_g"], 512)
    seg = lambda w, name: _matmul(xn, w, 512, name).reshape(b, t, -1)
    h_nsa, h_lru, h_gla, h_mla = (seg(w_nsa, "in_nsa"), seg(w_lru, "in_lru"), seg(w_gla, "in_gla"),
                                  seg(w_mla, "in_mla"))
    nsa_tabs, mla_tabs = tabs
    tq = 128
    qn, qr, rows, win, gate = _nsa_prep(h_nsa, nsa_tabs, p["nsa_q_g"], p["nsa_k_g"], tq)
    comp = _compress(rows, p["nsa_cmp_w"])
    o_cmp, sel = _nsa_cmp(qn, comp, tq)
    br_nsa = _nsa_attn(qr, sel, rows, win, o_cmp, gate, h_nsa, tq)
    br_lru, h_last, conv_new = _lru_prompt(h_lru, p, 256)
    br_gla, gla_state = _gla_prompt(h_gla, p)
    qcat, lat = _mla_prep(h_mla, mla_tabs, p, tq)
    br_mla = _mla_attn(qcat, lat, p["mla_w_uv"], h_mla[..., 640:1152], 256)
    flat = lambda a: a.reshape(n, BRANCH_W)
    merged = _merge(xn, [flat(br_nsa), flat(br_lru), flat(br_gla), flat(br_mla)], w_merge,
                    p["w_branch"].astype(BF16), 256)
    y = _outproj(merged, p["w_out"].astype(BF16), x.reshape(n, d), 256).reshape(b, t, d)
    keep = min(WINDOW, t)
    states = (rows.reshape(b, t, 4, NSA_HD), win[:, t - keep:].reshape(b, keep, 2, NSA_HD),
              h_last.reshape(b, BRANCH_W), conv_new, gla_state, lat)
    return y, states


def _sample_layer(x, p, layer, caches, tabs):
    s, t, d = x.shape
    cache_nsa_kv, cache_nsa_win, state_lru_h, state_lru_conv, state_gla, cache_mla, page_table = caches
    w_nsa, w_lru, w_gla, w_mla, w_merge = _input_weights(p["w_in"])
    x2 = x.reshape(s, d)
    xn = _rmsnorm_bf16(x2, p["norm_g"], s)
    h_nsa = _matmul(xn, w_nsa, s, "in_nsa_s")
    h_lru = _matmul(xn, w_lru, s, "in_lru_s")
    h_gla = _matmul(xn, w_gla, s, "in_gla_s")
    h_mla = _matmul(xn, w_mla, s, "in_mla_s")
    nsa_tabs, mla_tabs = tabs
    qn, qr, rows, win, gate = _nsa_prep(h_nsa[None], nsa_tabs, p["nsa_q_g"], p["nsa_k_g"], s)
    heads = lambda a: a.reshape(s, NSA_HEADS, NSA_HD)
    o_cmp, sel = _nsa_cmp_sample(cache_nsa_kv, layer, page_table, heads(qn[0]), p["nsa_cmp_w"], 8)
    kvr = jnp.concatenate([rows[0], win[0]], axis=1).reshape(s, 1, 384)
    o_slc, o_win, new_win = _nsa_slc_sample(cache_nsa_kv, cache_nsa_win, layer, page_table,
                                            sel.reshape(s, 128), heads(qr[0]), kvr)
    flat = lambda a: a.reshape(s, BRANCH_W)
    br_nsa = _nsa_post_sample(flat(o_cmp), flat(o_slc), flat(o_win), gate[0], h_nsa[:, 1024:1536])
    br_lru, h_new = _lru_sample(h_lru, state_lru_conv[layer], state_lru_h[layer], p)
    conv_new = jnp.concatenate([state_lru_conv[layer][:, 1:], h_lru[:, None, 0:512]], axis=1)
    br_gla, gla_state = _gla_sample(h_gla, state_gla, layer, p)
    qcat, lat = _mla_prep(h_mla[None], mla_tabs, p, s)
    o_lat = _mla_sample_attn(cache_mla, layer, page_table, jnp.swapaxes(qcat[0], 0, 1),
                             lat[0].reshape(s, 1, MLA_ROW), 8)
    br_mla = _mla_post_sample(o_lat, p["mla_w_uv"], h_mla[:, 640:1152])
    merged = _merge(xn, [br_nsa, br_lru, br_gla, br_mla], w_merge, p["w_branch"].astype(BF16), s)
    y = _outproj(merged, p["w_out"].astype(BF16), x2, s).reshape(s, t, d)
    states = (rows[0].reshape(s, 1, 4, NSA_HD), new_win[0], h_new, conv_new, gla_state,
              lat[0].reshape(s, 1, MLA_ROW))
    return y, states


def kernel(x_prompt, x_sample, cache_nsa_kv, cache_nsa_win, state_lru_h, state_lru_conv, state_gla, cache_mla,
           page_table, norm_g, w_in, nsa_q_g, nsa_k_g, nsa_cmp_w, lru_conv_w, lru_conv_b, lru_wa, lru_ba, lru_wx,
           lru_bx, lru_lam, gla_w_a2, gla_b_a, gla_out_g, mla_cq_g, mla_w_uq, mla_q_g, mla_ckv_g, mla_kr_g,
           mla_w_uk, mla_w_uv, w_branch, w_out):
    params = dict(norm_g=norm_g, w_in=w_in, nsa_q_g=nsa_q_g, nsa_k_g=nsa_k_g, nsa_cmp_w=nsa_cmp_w,
                  lru_conv_w=lru_conv_w, lru_conv_b=lru_conv_b, lru_wa=lru_wa, lru_ba=lru_ba, lru_wx=lru_wx,
                  lru_bx=lru_bx, lru_lam=lru_lam, gla_w_a2=gla_w_a2, gla_b_a=gla_b_a, gla_out_g=gla_out_g,
                  mla_cq_g=mla_cq_g, mla_w_uq=mla_w_uq, mla_q_g=mla_q_g, mla_ckv_g=mla_ckv_g,
                  mla_kr_g=mla_kr_g, mla_w_uk=mla_w_uk, mla_w_uv=mla_w_uv, w_branch=w_branch, w_out=w_out)
    depth = w_in.shape[0]
    assert x_prompt.shape[2] == N_BRANCH * BRANCH_W and x_sample.shape[1] == 1
    t = x_prompt.shape[1]
    n_seq, n_pages = page_table.shape
    past_len = n_pages * PAGE_SIZE
    assert t % 256 == 0 and past_len % SLC_BLOCK == 0 and past_len >= WINDOW
    caches = (cache_nsa_kv, cache_nsa_win, state_lru_h, state_lru_conv, state_gla, cache_mla, page_table)
    tabs_p = _rope_tables(jnp.arange(t))
    tabs_s = _rope_tables(jnp.full((n_seq,), past_len))
    y_p, y_s = x_prompt, x_sample
    sp, ss = [], []
    for layer in range(depth):
        p = _layer_params(params, layer)
        y_p, st = _prompt_layer(y_p, p, tabs_p)
        sp.append(st)
        y_s, st = _sample_layer(y_s, p, layer, caches, tabs_s)
        ss.append(st)
    stack = lambda lst, i: jnp.stack([s[i] for s in lst])
    return (y_p, y_s, stack(sp, 0), stack(ss, 0), stack(sp, 1), stack(ss, 1), stack(sp, 2), stack(ss, 2),
            stack(sp, 3), stack(ss, 3), stack(sp, 4), stack(ss, 4), stack(sp, 5), stack(ss, 5))
```

```python
import functools

import numpy as np
import jax
import jax.numpy as jnp
from jax import lax
from jax.experimental import pallas as pl
from jax.experimental.pallas import tpu as pltpu

F32 = jnp.float32
BF16 = jnp.bfloat16
NEG = -1e30

N_BRANCH = 4
NSA_HEADS = 8
NSA_HD = 64
CMP_BLOCK = 32
CMP_STRIDE = 16
SLC_BLOCK = 64
SLC_TOPN = 16
WINDOW = 512
PAGE_SIZE = 128
LRU_BLOCKS = 8
CONV_W = 4
LRU_C = 8.0
GLA_HEADS = 4
GLA_DK = 64
GLA_DV = 128
GLA_RANK = 16
GLA_TAU = 16.0
GLA_CHUNK = 128
MLA_HEADS = 4
MLA_NOPE = 128
MLA_ROPE = 64
MLA_VHD = 128
MLA_Q_LORA = 384
MLA_KV_LORA = 128
MLA_ROW = MLA_KV_LORA + MLA_ROPE
ROPE_THETA = 10000.0
NORM_EPS = 1e-6

BRANCH_W = 512
NSA_SCALE = NSA_HD ** -0.5
MLA_SCALE = (MLA_NOPE + MLA_ROPE) ** -0.5

SEG_NSA = 1536
SEG_LRU = 1024
SEG_GLA = 1664
SEG_MLA = 1152

VMEM_LIMIT = 48 * 1024 * 1024
SAMPLE_PAGES_PER_STEP = 32
CMP_GROUP_PAGES = 16


def _cparams(*sem):
    return pltpu.CompilerParams(dimension_semantics=sem, vmem_limit_bytes=VMEM_LIMIT)


def _dot(a, b):
    return jnp.dot(a.astype(BF16), b.astype(BF16), preferred_element_type=F32)


def _dot_nt(a, b):
    return lax.dot_general(a.astype(BF16), b.astype(BF16), (((1,), (1,)), ((), ())),
                           preferred_element_type=F32)


def _dot_tn(a, b):
    return lax.dot_general(a.astype(BF16), b.astype(BF16), (((0,), (0,)), ((), ())),
                           preferred_element_type=F32)


def _split3(a):
    a0 = a.astype(BF16)
    r = a - a0.astype(F32)
    a1 = r.astype(BF16)
    a2 = (r - a1.astype(F32)).astype(BF16)
    return a0, a1, a2


def _dot_x3(a, b_exact):
    a0, a1, a2 = _split3(a)
    d = functools.partial(jnp.dot, preferred_element_type=F32)
    return d(a0, b_exact) + d(a1, b_exact) + d(a2, b_exact)


def _x3_dot(a_exact, b):
    b0, b1, b2 = _split3(b)
    d = functools.partial(jnp.dot, preferred_element_type=F32)
    return d(a_exact, b0) + d(a_exact, b1) + d(a_exact, b2)


def _rope64(x, cos, sin_signed):
    w = x.shape[1]
    lane = lax.broadcasted_iota(jnp.int32, x.shape, 1)
    swapped = jnp.where((lane & 63) < 32, pltpu.roll(x, w - 32, 1), pltpu.roll(x, 32, 1))
    return x * cos + swapped * sin_signed


def _heads_to_rows(x, n_heads, hd):
    return jnp.concatenate([x[:, hd * h:hd * (h + 1)] for h in range(n_heads)], axis=0)


def _rows_to_heads(x, n_heads):
    t = x.shape[0] // n_heads
    return jnp.concatenate([x[t * h:t * (h + 1)] for h in range(n_heads)], axis=1)


def _silu(z):
    return z * jax.nn.sigmoid(z)


def _norm_kernel(x_ref, g_ref, o_ref):
    x = x_ref[...]
    ms = jnp.mean(x * x, axis=-1, keepdims=True)
    o_ref[...] = (x * lax.rsqrt(ms + NORM_EPS) * g_ref[...]).astype(o_ref.dtype)


def _rmsnorm_bf16(x, g, tm):
    n, d = x.shape
    return pl.pallas_call(
        _norm_kernel, grid=(n // tm,),
        in_specs=[pl.BlockSpec((tm, d), lambda i: (i, 0)), pl.BlockSpec((1, d), lambda i: (0, 0))],
        out_specs=pl.BlockSpec((tm, d), lambda i: (i, 0)),
        out_shape=jax.ShapeDtypeStruct((n, d), BF16),
        compiler_params=_cparams("parallel"), name="rmsnorm")(x, g.reshape(1, d))


def _mm_kernel(x_ref, w_ref, o_ref):
    o_ref[...] = jnp.dot(x_ref[...], w_ref[...], preferred_element_type=F32)


def _matmul(x, w, tm, name):
    n, k = x.shape
    c = w.shape[1]
    return pl.pallas_call(
        _mm_kernel, grid=(n // tm,),
        in_specs=[pl.BlockSpec((tm, k), lambda i: (i, 0)), pl.BlockSpec((k, c), lambda i: (0, 0))],
        out_specs=pl.BlockSpec((tm, c), lambda i: (i, 0)),
        out_shape=jax.ShapeDtypeStruct((n, c), F32),
        compiler_params=_cparams("parallel"), name=name)(x, w)


def _merge_kernel(xn_ref, b0_ref, b1_ref, b2_ref, b3_ref, wm_ref, wb_ref, o_ref, acc_ref):
    n = pl.program_id(1)

    @pl.when(n == 0)
    def _():
        acc_ref[...] = jnp.zeros_like(acc_ref)

    gate = jax.nn.sigmoid(jnp.dot(xn_ref[...], wm_ref[...], preferred_element_type=F32))
    for k, b_ref in enumerate((b0_ref, b1_ref, b2_ref, b3_ref)):
        @pl.when(n == k)
        def _(b_ref=b_ref):
            proj = jnp.dot(b_ref[...], wb_ref[0], preferred_element_type=F32)
            acc_ref[...] += gate * proj

    @pl.when(n == N_BRANCH - 1)
    def _():
        o_ref[...] = acc_ref[...].astype(o_ref.dtype)


def _merge(xn, branches, w_merge, w_branch, tm):
    n, d = xn.shape
    bw = branches[0].shape[1]
    bspec = pl.BlockSpec((tm, bw), lambda i, j: (i, 0))
    return pl.pallas_call(
        _merge_kernel, grid=(n // tm, N_BRANCH),
        in_specs=[pl.BlockSpec((tm, d), lambda i, j: (i, 0)), bspec, bspec, bspec, bspec,
                  pl.BlockSpec((d, d), lambda i, j: (0, j)),
                  pl.BlockSpec((1, bw, d), lambda i, j: (j, 0, 0))],
        out_specs=pl.BlockSpec((tm, d), lambda i, j: (i, 0)),
        out_shape=jax.ShapeDtypeStruct((n, d), BF16),
        scratch_shapes=[pltpu.VMEM((tm, d), F32)],
        compiler_params=_cparams("parallel", "arbitrary"), name="merge")(xn, *branches, w_merge, w_branch)


def _outproj_kernel(m_ref, w_ref, x_ref, o_ref):
    o_ref[...] = x_ref[...] + jnp.dot(m_ref[...], w_ref[...], preferred_element_type=F32)


def _outproj(merged, w_out, x, tm):
    n, d = x.shape
    return pl.pallas_call(
        _outproj_kernel, grid=(n // tm,),
        in_specs=[pl.BlockSpec((tm, d), lambda i: (i, 0)), pl.BlockSpec((d, d), lambda i: (0, 0)),
                  pl.BlockSpec((tm, d), lambda i: (i, 0))],
        out_specs=pl.BlockSpec((tm, d), lambda i: (i, 0)),
        out_shape=jax.ShapeDtypeStruct((n, d), F32),
        compiler_params=_cparams("parallel"), name="outproj")(merged, w_out, x)


def _nsa_prep_kernel(h_ref, cq_ref, sq_ref, ckv_ref, skv_ref, qg_ref, kvg_ref, kvm_ref, bdq_ref, bdkv_ref,
                     qn_ref, qr_ref, rows_ref, win_ref, gate_ref):
    h = h_ref[0]
    q = h[:, 0:512]
    kv = h[:, 512:896]
    ms_q = _dot_x3(q * q, bdq_ref[...]) * (1.0 / NSA_HD)
    qn = q * lax.rsqrt(ms_q + NORM_EPS) * qg_ref[...]
    qn_ref[0] = qn
    qr_ref[0] = _rope64(qn, cq_ref[...], sq_ref[...])
    ms_kv = _dot_x3(kv * kv, bdkv_ref[...]) * (1.0 / NSA_HD)
    kvn = kv * lax.rsqrt(ms_kv + NORM_EPS) * kvg_ref[...]
    kvn = jnp.where(kvm_ref[...] > 0.5, kvn, kv)
    kvr = _rope64(kvn, ckv_ref[...], skv_ref[...])
    rows_ref[0] = kvr[:, 0:256]
    win_ref[0] = kvr[:, 256:384]
    gate_ref[0] = jax.nn.sigmoid(h[:, 896:1024])


def _block_diag_ones(width, group):
    idx = np.arange(width) // group
    return jnp.asarray((idx[:, None] == idx[None, :]).astype(np.float32), dtype=BF16)


def _nsa_prep(h, tabs, q_gain, k_gain, tq):
    b, t, _ = h.shape
    cq, sq, ckv, skv = tabs
    one = jnp.ones((NSA_HD,), F32)
    kvg = jnp.concatenate([k_gain[0], one, k_gain[1], one, k_gain[2], one]).reshape(1, 384)
    kvm = jnp.concatenate([one, 0 * one, one, 0 * one, one, 0 * one]).reshape(1, 384)
    qg = jnp.tile(q_gain, NSA_HEADS).reshape(1, 512)
    tile = lambda w: pl.BlockSpec((1, tq, w), lambda bi, i: (bi, i, 0))
    tab = lambda w: pl.BlockSpec((tq, w), lambda bi, i: (i, 0))
    const = lambda r, w: pl.BlockSpec((r, w), lambda bi, i: (0, 0))
    return pl.pallas_call(
        _nsa_prep_kernel, grid=(b, t // tq),
        in_specs=[tile(SEG_NSA), tab(512), tab(512), tab(384), tab(384), const(1, 512), const(1, 384),
                  const(1, 384), const(512, 512), const(384, 384)],
        out_specs=[tile(512), tile(512), tile(256), tile(128), tile(128)],
        out_shape=[jax.ShapeDtypeStruct((b, t, 512), F32), jax.ShapeDtypeStruct((b, t, 512), F32),
                   jax.ShapeDtypeStruct((b, t, 256), F32), jax.ShapeDtypeStruct((b, t, 128), F32),
                   jax.ShapeDtypeStruct((b, t, 128), F32)],
        compiler_params=_cparams("parallel", "parallel"), name="nsa_prep",
    )(h, cq, sq, ckv, skv, qg, kvg, kvm, _block_diag_ones(512, 64), _block_diag_ones(384, 64))


def _rope_base(pos):
    inv = ROPE_THETA ** (-jnp.arange(0, 64, 2, dtype=F32) / 64)
    ang = pos.astype(F32)[:, None] * inv[None, :]
    c = jnp.cos(ang)
    s = jnp.sin(ang)
    return jnp.concatenate([c, c], axis=-1), jnp.concatenate([-s, s], axis=-1)


def _rope_tables(pos):
    c, s = _rope_base(pos)
    one = jnp.ones_like(c)
    zero = jnp.zeros_like(s)
    nsa = (jnp.tile(c, (1, NSA_HEADS)), jnp.tile(s, (1, NSA_HEADS)),
           jnp.concatenate([one, one, c, one, c, one], axis=1),
           jnp.concatenate([zero, zero, s, zero, s, zero], axis=1))
    mla = (jnp.tile(c, (1, MLA_HEADS)), jnp.tile(s, (1, MLA_HEADS)),
           jnp.concatenate([c, one], axis=1), jnp.concatenate([s, zero], axis=1))
    return nsa, mla


def _compress_kernel(rows_ref, w1_ref, w2_ref, comp_ref):
    t = rows_ref.shape[1]
    n = t // CMP_STRIDE
    first = jnp.zeros((n, 128), F32)
    second = jnp.zeros((n, 128), F32)
    for j in range(CMP_STRIDE):
        r = rows_ref[0, pl.ds(j, n, stride=CMP_STRIDE), :]
        first = first + r * w1_ref[j:j + 1, :]
        second = second + r * w2_ref[j:j + 1, :]
    nxt = pltpu.roll(second, n - 1, 0)
    row = lax.broadcasted_iota(jnp.int32, (n, 128), 0)
    comp_ref[0] = jnp.where(row < n - 1, first + nxt, 0.0)


def _cmp_weights(w_pos):
    w1 = jnp.concatenate([w_pos[0, :CMP_STRIDE], w_pos[1, :CMP_STRIDE]], axis=1)
    w2 = jnp.concatenate([w_pos[0, CMP_STRIDE:], w_pos[1, CMP_STRIDE:]], axis=1)
    return w1, w2


def _compress(rows, w_pos):
    b, t, _ = rows.shape
    w1, w2 = _cmp_weights(w_pos)
    n = t // CMP_STRIDE
    return pl.pallas_call(
        _compress_kernel, grid=(b,),
        in_specs=[pl.BlockSpec((1, t, 128), lambda bi: (bi, 0, 0)),
                  pl.BlockSpec((CMP_STRIDE, 128), lambda bi: (0, 0)),
                  pl.BlockSpec((CMP_STRIDE, 128), lambda bi: (0, 0))],
        out_specs=pl.BlockSpec((1, n, 128), lambda bi: (bi, 0, 0)),
        out_shape=jax.ShapeDtypeStruct((b, n, 128), F32),
        compiler_params=_cparams("parallel"), name="nsa_compress")(rows, w1, w2)


def _score_matrix(n_cmp, n_slc):
    m = np.zeros((n_cmp, n_slc), np.float32)
    per = SLC_BLOCK // CMP_STRIDE
    for n in range(n_cmp):
        for s in (n, n + 1):
            if s // per < n_slc:
                m[n, s // per] += 1.0
    return jnp.asarray(m, dtype=BF16)


def _nsa_cmp_kernel(qn_ref, comp_ref, m_ref, o_ref, sel_ref, *, tq):
    i = pl.program_id(1)
    qh = _heads_to_rows(qn_ref[0], NSA_HEADS, NSA_HD)
    comp = comp_ref[0]
    nc = comp.shape[0]
    s = _dot_nt(qh, comp[:, 0:64]) * NSA_SCALE
    qpos = i * tq + lax.broadcasted_iota(jnp.int32, (tq, nc), 0)
    blk_end = lax.broadcasted_iota(jnp.int32, (tq, nc), 1) * CMP_STRIDE + (CMP_BLOCK - 1)
    mask = (blk_end <= qpos)[None]
    s3 = jnp.where(mask, s.reshape(NSA_HEADS, tq, nc), NEG)
    mx = jnp.max(s3, axis=-1, keepdims=True)
    e = jnp.where(mask, jnp.exp(s3 - mx), 0.0)
    p3 = e / jnp.maximum(jnp.sum(e, axis=-1, keepdims=True), 1e-30)
    o = _dot(p3.reshape(NSA_HEADS * tq, nc), comp[:, 64:128])
    o_ref[0] = _rows_to_heads(o, NSA_HEADS)
    score = _dot_x3(jnp.sum(p3, axis=0), m_ref[...])
    nsb = score.shape[1]
    blk = lax.broadcasted_iota(jnp.int32, (tq, nsb), 1)
    qp = i * tq + lax.broadcasted_iota(jnp.int32, (tq, nsb), 0)
    cur = jnp.right_shift(qp, 6)
    valid = blk * SLC_BLOCK <= qp
    forced = (blk == 0) | (blk == cur) | (blk == cur - 1)
    sc = jnp.where(valid, jnp.where(forced, 1e30, score), -1e30)
    rank = jnp.zeros((tq, nsb), F32)
    for b in range(nsb):
        col = sc[:, b:b + 1]
        rank = rank + jnp.where((col > sc) | ((col == sc) & (blk > b)), 1.0, 0.0)
    sel_ref[0] = jnp.where((rank < SLC_TOPN) & valid, 1.0, 0.0)


def _nsa_cmp(qn, comp, tq):
    b, t, _ = qn.shape
    nc = comp.shape[1]
    nsb = t // SLC_BLOCK
    return pl.pallas_call(
        functools.partial(_nsa_cmp_kernel, tq=tq), grid=(b, t // tq),
        in_specs=[pl.BlockSpec((1, tq, 512), lambda bi, i: (bi, i, 0)),
                  pl.BlockSpec((1, nc, 128), lambda bi, i: (bi, 0, 0)),
                  pl.BlockSpec((nc, nsb), lambda bi, i: (0, 0))],
        out_specs=[pl.BlockSpec((1, tq, 512), lambda bi, i: (bi, i, 0)),
                   pl.BlockSpec((1, tq, nsb), lambda bi, i: (bi, i, 0))],
        out_shape=[jax.ShapeDtypeStruct((b, t, 512), F32), jax.ShapeDtypeStruct((b, t, nsb), F32)],
        compiler_params=_cparams("parallel", "parallel"), name="nsa_cmp")(qn, comp, _score_matrix(nc, nsb))


def _gate_expand(c):
    g = np.zeros((128, 512), np.float32)
    for h in range(NSA_HEADS):
        g[3 * h + c, NSA_HD * h:NSA_HD * (h + 1)] = 1.0
    return g


def _gate_mats():
    return jnp.asarray(np.stack([_gate_expand(c) for c in range(3)]), dtype=BF16)


def _nsa_combine(o_cmp, o_slc, o_win, gate, gx_ref, z):
    g0 = _dot_x3(gate, gx_ref[0])
    g1 = _dot_x3(gate, gx_ref[1])
    g2 = _dot_x3(gate, gx_ref[2])
    return ((g0 * o_cmp + g1 * o_slc + g2 * o_win) * _silu(z)).astype(BF16)


def _flash_reset(m_s, l_s, acc_s):
    m_s[...] = jnp.full_like(m_s, NEG)
    l_s[...] = jnp.zeros_like(l_s)
    acc_s[...] = jnp.zeros_like(acc_s)


def _flash_update(s_t, v, m_s, l_s, acc_s):
    m_new = jnp.maximum(m_s[...], jnp.max(s_t, axis=0, keepdims=True))
    alpha = jnp.exp(m_s[...] - m_new)
    p = jnp.exp(s_t - m_new)
    l_s[...] = alpha * l_s[...] + jnp.sum(p, axis=0, keepdims=True)
    acc_s[...] = alpha * acc_s[...] + _dot_tn(v, p)
    m_s[...] = m_new


def _nsa_attn_kernel(qr_ref, sel_ref, rows_ref, win_ref, ocmp_ref, gate_ref, gx_ref, z_ref, out_ref,
                     m_s, l_s, acc_s, *, tq):
    i = pl.program_id(1)
    q = qr_ref[0] * NSA_SCALE
    q_t = jnp.concatenate([q[:, NSA_HD * h:NSA_HD * (h + 1)].T for h in range(NSA_HEADS)], axis=1).astype(BF16)
    sel_t = sel_ref[0].T.astype(BF16)
    nsb = sel_t.shape[0]
    koff = lax.broadcasted_iota(jnp.int32, (tq, tq), 0)
    qpos = i * tq + lax.broadcasted_iota(jnp.int32, (tq, tq), 1)
    stats = (m_s, l_s, acc_s)

    def scores(k, mask):
        bias = jnp.where(mask, 0.0, NEG)
        return _dot(k, q_t) + jnp.concatenate([bias] * NSA_HEADS, axis=1)

    def finish():
        o_t = acc_s[...] / l_s[...]
        return jnp.concatenate([o_t[:, tq * h:tq * (h + 1)].T for h in range(NSA_HEADS)], axis=1)

    def slc_tile(j, carry):
        start = pl.multiple_of(j * tq, tq)
        k = rows_ref[0, pl.ds(start, tq), 128:192]
        v = rows_ref[0, pl.ds(start, tq), 192:256]
        kpos = start + koff
        expand = (lax.broadcasted_iota(jnp.int32, (tq, nsb), 1)
                  == jnp.right_shift(start + lax.broadcasted_iota(jnp.int32, (tq, nsb), 0), 6))
        selk = jnp.dot(jnp.where(expand, 1.0, 0.0).astype(BF16), sel_t, preferred_element_type=F32)
        _flash_update(scores(k, (kpos <= qpos) & (selk > 0.5)), v, *stats)
        return carry

    _flash_reset(*stats)
    lax.fori_loop(0, i + 1, slc_tile, 0)
    o_slc = finish()

    def win_tile(n, carry):
        start = pl.multiple_of((i - n) * tq, tq)
        k = win_ref[0, pl.ds(start, tq), 0:64]
        v = win_ref[0, pl.ds(start, tq), 64:128]
        kpos = start + koff
        _flash_update(scores(k, (kpos <= qpos) & (kpos > qpos - WINDOW)), v, *stats)
        return carry

    _flash_reset(*stats)
    lax.fori_loop(0, jnp.minimum(i, WINDOW // tq) + 1, win_tile, 0)
    o_win = finish()
    out_ref[0] = _nsa_combine(ocmp_ref[0], o_slc, o_win, gate_ref[0], gx_ref, z_ref[0])


def _nsa_attn(qr, sel, rows, win, o_cmp, gate, h, tq):
    b, t, _ = qr.shape
    nsb = sel.shape[2]
    tile = lambda w: pl.BlockSpec((1, tq, w), lambda bi, i: (bi, i, 0))
    return pl.pallas_call(
        functools.partial(_nsa_attn_kernel, tq=tq), grid=(b, t // tq),
        in_specs=[tile(512), tile(nsb),
                  pl.BlockSpec((1, t, 256), lambda bi, i: (bi, 0, 0)),
                  pl.BlockSpec((1, t, 128), lambda bi, i: (bi, 0, 0)),
                  tile(512), tile(128),
                  pl.BlockSpec((3, 128, 512), lambda bi, i: (0, 0, 0)),
                  pl.BlockSpec((1, tq, 512), lambda bi, i: (bi, i, 2))],
        out_specs=tile(512),
        out_shape=jax.ShapeDtypeStruct((b, t, 512), BF16),
        scratch_shapes=[pltpu.VMEM((1, NSA_HEADS * tq), F32), pltpu.VMEM((1, NSA_HEADS * tq), F32),
                        pltpu.VMEM((NSA_HD, NSA_HEADS * tq), F32)],
        compiler_params=_cparams("parallel", "arbitrary"), name="nsa_attn",
    )(qr, sel, rows, win, o_cmp, gate, _gate_mats(), h)


def _nsa_cmp_sample_kernel(pt_ref, *refs, n_in, n_pages, past_len):
    page_refs = refs[:n_in]
    qn_ref, wt_ref, gt_ref, m_ref, o_ref, sel_ref, fk_s, fv_s, sk_s, sv_s = refs[n_in:]
    c = pl.program_id(1)
    per_page = PAGE_SIZE // CMP_STRIDE
    per_step = n_in * per_page
    group = gt_ref.shape[0] // per_page
    for g0 in range(0, n_in, group):
        weighted = [[], [], [], []]
        for p_ref in page_refs[g0:g0 + group]:
            k_t = p_ref[0, 0, 0]
            v_t = p_ref[0, 0, 1]
            weighted[0].append((k_t * wt_ref[0]).astype(BF16))
            weighted[1].append((k_t * wt_ref[1]).astype(BF16))
            weighted[2].append((v_t * wt_ref[2]).astype(BF16))
            weighted[3].append((v_t * wt_ref[3]).astype(BF16))
        off = pl.multiple_of(c * per_step + g0 * per_page, group * per_page)
        for dst, parts in zip((fk_s, sk_s, fv_s, sv_s), weighted):
            dst[pl.ds(off, group * per_page), :] = lax.dot_general(
                gt_ref[...], jnp.concatenate(parts, axis=1), (((1,), (1,)), ((), ())),
                preferred_element_type=F32)

    @pl.when(c == pl.num_programs(1) - 1)
    def _():
        nc = n_pages * per_page
        sk_s[pl.ds(nc, 8), :] = jnp.zeros((8, NSA_HD), F32)
        sv_s[pl.ds(nc, 8), :] = jnp.zeros((8, NSA_HD), F32)
        ck = fk_s[pl.ds(0, nc), :] + sk_s[pl.ds(1, nc), :]
        cv = fv_s[pl.ds(0, nc), :] + sv_s[pl.ds(1, nc), :]
        q = qn_ref[0]
        s = _dot_nt(q, ck) * NSA_SCALE
        blk_end = lax.broadcasted_iota(jnp.int32, (NSA_HEADS, nc), 1) * CMP_STRIDE + (CMP_BLOCK - 1)
        mask = blk_end <= past_len
        s = jnp.where(mask, s, NEG)
        e = jnp.where(mask, jnp.exp(s - jnp.max(s, axis=-1, keepdims=True)), 0.0)
        p = e / jnp.maximum(jnp.sum(e, axis=-1, keepdims=True), 1e-30)
        o_ref[0] = _dot(p, cv)
        imp = jnp.sum(p, axis=0, keepdims=True)
        score = _dot_x3(imp, m_ref[...])
        nsb = score.shape[1]
        lane = lax.broadcasted_iota(jnp.int32, (nsb, nsb), 1)
        sub = lax.broadcasted_iota(jnp.int32, (nsb, nsb), 0)
        forced_row = (lane == 0) | (lane == nsb - 1)
        sc_row = jnp.where(forced_row, 1e30, jnp.broadcast_to(score, (nsb, nsb)))
        eye = lane == sub
        sc_col = jnp.sum(jnp.where(eye, sc_row, 0.0), axis=1, keepdims=True)
        beats_row = (sc_col > sc_row) | ((sc_col == sc_row) & (sub < lane))
        rank_row = jnp.sum(jnp.where(beats_row, 1.0, 0.0), axis=0, keepdims=True)
        sel_row = rank_row < (SLC_TOPN - 1)
        beats_col = (sc_row > sc_col) | ((sc_row == sc_col) & (lane < sub))
        rank_col = jnp.sum(jnp.where(beats_col, 1.0, 0.0), axis=1, keepdims=True)
        sel_col = rank_col < (SLC_TOPN - 1)
        pos_col = jnp.sum(jnp.where(sel_row & (lane < sub), 1.0, 0.0), axis=1, keepdims=True)
        slot = lax.broadcasted_iota(jnp.int32, (nsb, 128), 1).astype(F32)
        blk_id = lax.broadcasted_iota(jnp.int32, (nsb, 128), 0).astype(F32)
        onehot = sel_col & (pos_col == slot)
        sel_ref[0] = jnp.sum(jnp.where(onehot, blk_id, 0.0), axis=0, keepdims=True).astype(jnp.int32)


def _nsa_cmp_sample(cache_kv_t, layer, page_table, qn, w_pos, pages_per_step):
    s, n_pages = page_table.shape
    past_len = n_pages * PAGE_SIZE
    per_page = PAGE_SIZE // CMP_STRIDE
    nc = n_pages * per_page
    nsb = past_len // SLC_BLOCK
    n_in = pages_per_step
    tile_t = lambda w: jnp.tile(w.T, (1, per_page))
    wt = jnp.stack([tile_t(w_pos[0, :CMP_STRIDE]), tile_t(w_pos[0, CMP_STRIDE:]),
                    tile_t(w_pos[1, :CMP_STRIDE]), tile_t(w_pos[1, CMP_STRIDE:])])
    group = min(n_in, CMP_GROUP_PAGES)
    assert n_in % group == 0
    col = np.arange(group * PAGE_SIZE)
    gt = jnp.asarray((np.arange(group * per_page)[:, None] == (col // CMP_STRIDE)[None, :]).astype(np.float32),
                     dtype=BF16)

    def page_spec(k):
        return pl.BlockSpec((1, 1, 2, NSA_HD, PAGE_SIZE),
                            lambda b, c, pt: (layer, pt[b, c * n_in + k], 0, 0, 0))

    const = lambda r, w: pl.BlockSpec((r, w), lambda b, c, pt: (0, 0))
    grid_spec = pltpu.PrefetchScalarGridSpec(
        num_scalar_prefetch=1, grid=(s, n_pages // n_in),
        in_specs=[page_spec(k) for k in range(n_in)]
        + [pl.BlockSpec((1, NSA_HEADS, NSA_HD), lambda b, c, pt: (b, 0, 0)),
           pl.BlockSpec((4, NSA_HD, PAGE_SIZE), lambda b, c, pt: (0, 0, 0)),
           const(group * per_page, group * PAGE_SIZE), const(nc, nsb)],
        out_specs=[pl.BlockSpec((1, NSA_HEADS, NSA_HD), lambda b, c, pt: (b, 0, 0)),
                   pl.BlockSpec((1, 1, 128), lambda b, c, pt: (b, 0, 0))],
        scratch_shapes=[pltpu.VMEM((nc, NSA_HD), F32), pltpu.VMEM((nc, NSA_HD), F32),
                        pltpu.VMEM((nc + 8, NSA_HD), F32), pltpu.VMEM((nc + 8, NSA_HD), F32)])
    return pl.pallas_call(
        functools.partial(_nsa_cmp_sample_kernel, n_in=n_in, n_pages=n_pages, past_len=past_len),
        grid_spec=grid_spec,
        out_shape=[jax.ShapeDtypeStruct((s, NSA_HEADS, NSA_HD), F32),
                   jax.ShapeDtypeStruct((s, 1, 128), jnp.int32)],
        compiler_params=_cparams("parallel", "arbitrary"), name="nsa_cmp_sample",
    )(page_table, *([cache_kv_t] * n_in), qn, wt, gt, _score_matrix(nc, nsb))


def _attend_with_new(q, s_past, v_t_parts, k_new, v_new):
    s_new = jnp.sum(q * k_new, axis=-1, keepdims=True)
    m = jnp.maximum(jnp.max(s_past, axis=-1, keepdims=True), s_new)
    p = jnp.exp(s_past - m)
    p_new = jnp.exp(s_new - m)
    acc = p_new * v_new
    off = 0
    for v_t in v_t_parts:
        n = v_t.shape[1]
        acc = acc + _dot_nt(p[:, off:off + n], v_t)
        off += n
    return acc / (jnp.sum(p, axis=-1, keepdims=True) + p_new)


def _nsa_slc_sample_kernel(sel_ref, pt_ref, *refs, n_sel, past_len):
    b = pl.program_id(0)
    blk_refs = refs[:n_sel]
    qr_ref, new_ref, winbuf_ref, oslc_ref, owin_ref, newwin_ref = refs[n_sel:]
    q = qr_ref[0] * NSA_SCALE
    new = new_ref[0]
    per_page = PAGE_SIZE // SLC_BLOCK
    lane = lax.broadcasted_iota(jnp.int32, (NSA_HEADS, PAGE_SIZE), 1)
    parts = []
    for k, r in enumerate(blk_refs):
        half = sel_ref[b, k] % per_page
        parts.append(jnp.where(jnp.right_shift(lane, 6) == half, _dot(q, r[0, 0, 0]), NEG))
    oslc_ref[0] = _attend_with_new(q, jnp.concatenate(parts, axis=1), [r[0, 0, 1] for r in blk_refs],
                                   new[:, 128:192], new[:, 192:256])
    lw = winbuf_ref.shape[4]
    kw_t = winbuf_ref[0, 0, 0]
    vw_t = winbuf_ref[0, 0, 1]
    k_pos = past_len - lw + lax.broadcasted_iota(jnp.int32, (NSA_HEADS, lw), 1)
    s = jnp.where(k_pos > past_len - WINDOW, _dot(q, kw_t), NEG)
    owin_ref[0] = _attend_with_new(q, s, [vw_t], new[:, 256:320], new[:, 320:384])
    eye = (lax.broadcasted_iota(jnp.int32, (NSA_HD, NSA_HD), 0)
           == lax.broadcasted_iota(jnp.int32, (NSA_HD, NSA_HD), 1))
    to_col = lambda r: jnp.sum(jnp.where(eye, r, 0.0), axis=1, keepdims=True)
    last = lax.broadcasted_iota(jnp.int32, (NSA_HD, lw), 1) == lw - 1
    newwin_ref[0, 0, 0] = jnp.where(last, to_col(new[:, 256:320]), pltpu.roll(kw_t, lw - 1, 1))
    newwin_ref[0, 0, 1] = jnp.where(last, to_col(new[:, 320:384]), pltpu.roll(vw_t, lw - 1, 1))


def _nsa_slc_sample(cache_kv_t, cache_win_t, layer, page_table, sel, qr, kvr):
    s, n_pages = page_table.shape
    past_len = n_pages * PAGE_SIZE
    lw = cache_win_t.shape[4]
    n_sel = SLC_TOPN - 1
    per_page = PAGE_SIZE // SLC_BLOCK

    def blk_spec(k):
        return pl.BlockSpec((1, 1, 2, NSA_HD, PAGE_SIZE),
                            lambda b, sl, pt: (layer, pt[b, sl[b, k] // per_page], 1, 0, 0))

    grid_spec = pltpu.PrefetchScalarGridSpec(
        num_scalar_prefetch=2, grid=(s,),
        in_specs=[blk_spec(k) for k in range(n_sel)]
        + [pl.BlockSpec((1, NSA_HEADS, NSA_HD), lambda b, sl, pt: (b, 0, 0)),
           pl.BlockSpec((1, 1, 384), lambda b, sl, pt: (b, 0, 0)),
           pl.BlockSpec((1, 1, 2, NSA_HD, lw), lambda b, sl, pt: (layer, b, 0, 0, 0))],
        out_specs=[pl.BlockSpec((1, NSA_HEADS, NSA_HD), lambda b, sl, pt: (b, 0, 0)),
                   pl.BlockSpec((1, NSA_HEADS, NSA_HD), lambda b, sl, pt: (b, 0, 0)),
                   pl.BlockSpec((1, 1, 2, NSA_HD, lw), lambda b, sl, pt: (0, b, 0, 0, 0))])
    return pl.pallas_call(
        functools.partial(_nsa_slc_sample_kernel, n_sel=n_sel, past_len=past_len),
        grid_spec=grid_spec,
        out_shape=[jax.ShapeDtypeStruct((s, NSA_HEADS, NSA_HD), F32),
                   jax.ShapeDtypeStruct((s, NSA_HEADS, NSA_HD), F32),
                   jax.ShapeDtypeStruct((1, s, 2, NSA_HD, lw), F32)],
        compiler_params=_cparams("parallel"), name="nsa_slc_sample",
    )(sel, page_table, *([cache_kv_t] * n_sel), qr, kvr, cache_win_t)


def _nsa_post_sample_kernel(ocmp_ref, oslc_ref, owin_ref, gate_ref, gx_ref, z_ref, out_ref):
    out_ref[...] = _nsa_combine(ocmp_ref[...], oslc_ref[...], owin_ref[...], gate_ref[...], gx_ref, z_ref[...])


def _nsa_post_sample(o_cmp, o_slc, o_win, gate, z):
    s = gate.shape[0]
    return pl.pallas_call(
        _nsa_post_sample_kernel, out_shape=jax.ShapeDtypeStruct((s, 512), BF16), name="nsa_post_sample",
    )(o_cmp, o_slc, o_win, gate, _gate_mats(), z)


def _lru_gates(u, wa_ref, ba_ref, wx_ref, bx_ref, lam_ref):
    r =jax.nn.sigmoid(_dot(u, wa_ref[...]) + ba_ref[...])
    ig = jax.nn.sigmoid(_dot(u, wx_ref[...]) + bx_ref[...])
    log_a = -LRU_C * r * jax.nn.softplus(-lam_ref[...])
    a = jnp.exp(log_a)
    b = jnp.sqrt(jnp.tanh(-log_a) * (a * a + 1.0)) * (ig * u)
    return a, b


def _lru_kernel(h_ref, cw_ref, cb_ref, wa_ref, ba_ref, wx_ref, bx_ref, lam_ref,
                br_ref, hlast_ref, conv_ref, xbuf, hc):
    t = pl.program_id(1)
    tt = h_ref.shape[1]

    @pl.when(t == 0)
    def _():
        xbuf[0:8, :] = jnp.zeros((8, BRANCH_W), F32)
        hc[...] = jnp.zeros_like(hc)

    x = h_ref[0, :, 0:512]
    z = h_ref[0, :, 512:1024]
    xbuf[8:8 + tt, :] = x
    u = cb_ref[...] + xbuf[pl.ds(8, tt), :] * cw_ref[3:4, :]
    for j in range(CONV_W - 1):
        u = u + xbuf[pl.ds(5 + j, tt), :] * cw_ref[j:j + 1, :]
    xbuf[0:8, :] = xbuf[tt:tt + 8, :]
    a, b = _lru_gates(u, wa_ref, ba_ref, wx_ref, bx_ref, lam_ref)
    row = lax.broadcasted_iota(jnp.int32, (tt, BRANCH_W), 0)
    d = 1
    while d < tt:
        keep = row >= d
        b = jnp.where(keep, a * pltpu.roll(b, d, 0) + b, b)
        a = jnp.where(keep, a * pltpu.roll(a, d, 0), a)
        d *= 2
    hseq = a * hc[...] + b
    hc[...] = hseq[tt - 1:tt, :]
    br_ref[0] = (hseq * _silu(z)).astype(BF16)

    @pl.when(t == pl.num_programs(1) - 1)
    def _():
        hlast_ref[0] = hseq[tt - 1:tt, :]
        conv_ref[0] = x[tt - (CONV_W - 1):tt, :]


def _block_diag(w):
    n, c, d = w.shape
    eye = jnp.eye(n, dtype=w.dtype)
    return (w[:, :, None, :] * eye[:, None, :, None]).reshape(n * c, n * d)


def _lru_weights(p):
    row = lambda v: v.reshape(1, BRANCH_W)
    return (p["lru_conv_w"], row(p["lru_conv_b"]), _block_diag(p["lru_wa"]).astype(BF16), row(p["lru_ba"]),
            _block_diag(p["lru_wx"]).astype(BF16), row(p["lru_bx"]), row(p["lru_lam"]))


def _lru_prompt(h, p, tt):
    b, t, _ = h.shape
    const = lambda r, w: pl.BlockSpec((r, w), lambda bi, i: (0, 0))
    return pl.pallas_call(
        _lru_kernel, grid=(b, t // tt),
        in_specs=[pl.BlockSpec((1, tt, SEG_LRU), lambda bi, i: (bi, i, 0)),
                  const(CONV_W, 512), const(1, 512), const(512, 512), const(1, 512), const(512, 512),
                  const(1, 512), const(1, 512)],
        out_specs=[pl.BlockSpec((1, tt, 512), lambda bi, i: (bi, i, 0)),
                   pl.BlockSpec((1, 1, 512), lambda bi, i: (bi, 0, 0)),
                   pl.BlockSpec((1, CONV_W - 1, 512), lambda bi, i: (bi, 0, 0))],
        out_shape=[jax.ShapeDtypeStruct((b, t, 512), BF16), jax.ShapeDtypeStruct((b, 1, 512), F32),
                   jax.ShapeDtypeStruct((b, CONV_W - 1, 512), F32)],
        scratch_shapes=[pltpu.VMEM((tt + 8, 512), F32), pltpu.VMEM((1, 512), F32)],
        compiler_params=_cparams("parallel", "arbitrary"), name="lru_prompt")(h, *_lru_weights(p))


def _lru_sample_kernel(h_ref, c0_ref, c1_ref, c2_ref, h0_ref, cw_ref, cb_ref, wa_ref, ba_ref, wx_ref, bx_ref,
                       lam_ref, br_ref, hnew_ref):
    x = h_ref[:, 0:512]
    z = h_ref[:, 512:1024]
    u = cb_ref[...] + x * cw_ref[3:4, :]
    for j, c_ref in enumerate((c0_ref, c1_ref, c2_ref)):
        u = u + c_ref[...] * cw_ref[j:j + 1, :]
    a, b = _lru_gates(u, wa_ref, ba_ref, wx_ref, bx_ref, lam_ref)
    hnew = a * h0_ref[...] + b
    hnew_ref[...] = hnew
    br_ref[...] = (hnew * _silu(z)).astype(BF16)


def _lru_sample(h, conv_prev_t, h0, p):
    s = h.shape[0]
    c0, c1, c2 = (conv_prev_t[j] for j in range(CONV_W - 1))
    return pl.pallas_call(
        _lru_sample_kernel,
        out_shape=[jax.ShapeDtypeStruct((s, 512), BF16), jax.ShapeDtypeStruct((s, 512), F32)],
        name="lru_sample")(h, c0, c1, c2, h0, *_lru_weights(p))


def _gla_constants(c):
    tri = np.tril(np.ones((c, c), np.float32))
    rows = [tri]
    masks = []
    t = np.arange(c)
    b = c // 2
    while b >= 1:
        ref_row = (t // (2 * b)) * 2 * b + b - 1
        rows.append(tri[ref_row])
        same = (t[:, None] // (2 * b)) == (t[None, :] // (2 * b))
        masks.append((same & ((t[:, None] // b) % 2 == 1) & ((t[None, :] // b) % 2 == 0)).astype(np.float32))
        b //= 2
    return (jnp.asarray(np.concatenate(rows, axis=0), dtype=BF16), jnp.asarray(np.stack(masks), dtype=F32))


def _gla_log_alpha(a_in, wa_ref, ba_ref):
    return jax.nn.log_sigmoid(_dot(a_in, wa_ref[...]) + ba_ref[...]) * (1.0 / GLA_TAU)


def _gla_out(o, gout_ref, z):
    outs = []
    for h in range(GLA_HEADS):
        oh = o[h]
        ms = jnp.mean(oh * oh, axis=-1, keepdims=True)
        outs.append(oh * lax.rsqrt(ms + NORM_EPS) * gout_ref[...])
    return (jnp.concatenate(outs, axis=1) * _silu(z)).astype(BF16)


def _gla_kernel(h_ref, wa_ref, ba_ref, gout_ref, cum_ref, lm_ref, br_ref, state_ref, st_s):
    ci = pl.program_id(1)
    c = h_ref.shape[1]
    n_lvl = lm_ref.shape[0]

    @pl.when(ci == 0)
    def _():
        st_s[...] = jnp.zeros_like(st_s)

    h = h_ref[0]
    q = h[:, 0:256] * (GLA_DK ** -0.5)
    k = h[:, 256:512]
    v = h[:, 512:1024]
    g = _gla_log_alpha(h[:, 1024:1152], wa_ref, ba_ref)
    cums = _x3_dot(cum_ref[...], g)
    bc = cums[0:c]
    b_end = bc[c - 1:c, :]
    q_in = q * jnp.exp(bc)
    k_end = k * jnp.exp(b_end - bc)
    aq = []
    ak = []
    for lv in range(n_lvl):
        ref = cums[(lv + 1) * c:(lv + 2) * c]
        aq.append(q * jnp.exp(jnp.minimum(bc - ref, 0.0)))
        ak.append(k * jnp.exp(jnp.minimum(ref - bc, 0.0)))
    eye = (lax.broadcasted_iota(jnp.int32, (c, c), 0) == lax.broadcasted_iota(jnp.int32, (c, c), 1))
    decay_end = jnp.exp(b_end)
    outs = []
    for hd in range(GLA_HEADS):
        ks = slice(GLA_DK * hd, GLA_DK * (hd + 1))
        vh = v[:, GLA_DV * hd:GLA_DV * (hd + 1)]
        att = jnp.where(eye, _dot_nt(q[:, ks], k[:, ks]), 0.0)
        for lv in range(n_lvl):
            att = att + lm_ref[lv] * _dot_nt(aq[lv][:, ks], ak[lv][:, ks])
        st = st_s[hd]
        outs.append(_dot_nt(q_in[:, ks], st) + _dot(att, vh))
        st_s[hd] = st * decay_end[:, ks] + _dot_tn(vh, k_end[:, ks])
    br_ref[0] = _gla_out(outs, gout_ref, h[:, 1152:1664])

    @pl.when(ci == pl.num_programs(1) - 1)
    def _():
        for hd in range(GLA_HEADS):
            state_ref[0, hd] = st_s[hd].T


def _gla_weights(p):
    wa = jnp.zeros((128, GLA_HEADS * GLA_DK), F32).at[:GLA_RANK].set(p["gla_w_a2"]).astype(BF16)
    return wa, p["gla_b_a"].reshape(1, -1), p["gla_out_g"].reshape(1, -1)


def _gla_prompt(h, p):
    b, t, _ = h.shape
    c = GLA_CHUNK
    cum, lm = _gla_constants(c)
    const2 = lambda r, w: pl.BlockSpec((r, w), lambda bi, i: (0, 0))
    return pl.pallas_call(
        _gla_kernel, grid=(b, t // c),
        in_specs=[pl.BlockSpec((1, c, SEG_GLA), lambda bi, i: (bi, i, 0)),
                  const2(128, 256), const2(1, 256), const2(1, GLA_DV), const2(cum.shape[0], c),
                  pl.BlockSpec(lm.shape, lambda bi, i: (0, 0, 0))],
        out_specs=[pl.BlockSpec((1, c, 512), lambda bi, i: (bi, i, 0)),
                   pl.BlockSpec((1, GLA_HEADS, GLA_DK, GLA_DV), lambda bi, i: (bi, 0, 0, 0))],
        out_shape=[jax.ShapeDtypeStruct((b, t, 512), BF16),
                   jax.ShapeDtypeStruct((b, GLA_HEADS, GLA_DK, GLA_DV), F32)],
        scratch_shapes=[pltpu.VMEM((GLA_HEADS, GLA_DV, GLA_DK), F32)],
        compiler_params=_cparams("parallel", "arbitrary"), name="gla_prompt")(h, *_gla_weights(p), cum, lm)


def _gla_sample_kernel(h_ref, s0_ref, wa_ref, ba_ref, gout_ref, br_ref, snew_ref):
    h = h_ref[0]
    q = h[:, 0:256] * (GLA_DK ** -0.5)
    k = h[:, 256:512]
    v = h[:, 512:1024]
    g = _gla_log_alpha(h[:, 1024:1152], wa_ref, ba_ref)
    decay = jnp.exp(g)
    eye = (lax.broadcasted_iota(jnp.int32, (GLA_DK, GLA_DK), 0)
           == lax.broadcasted_iota(jnp.int32, (GLA_DK, GLA_DK), 1))
    to_col = lambda r: jnp.sum(jnp.where(eye, r, 0.0), axis=1, keepdims=True)
    outs = []
    for hd in range(GLA_HEADS):
        ks = slice(GLA_DK * hd, GLA_DK * (hd + 1))
        vh = v[:, GLA_DV * hd:GLA_DV * (hd + 1)]
        s0 = s0_ref[0, 0, hd]
        qk = jnp.sum(q[:, ks] * k[:, ks], axis=-1, keepdims=True)
        outs.append(_dot(q[:, ks] * decay[:, ks], s0) + qk * vh)
        snew_ref[0, hd] = to_col(decay[:, ks]) * s0 + to_col(k[:, ks]) * vh
    br_ref[0] = _gla_out(outs, gout_ref, h[:, 1152:1664])


def _gla_sample(h, state, layer, p):
    s = h.shape[0]
    const2 = lambda r, w: pl.BlockSpec((r, w), lambda b: (0, 0))
    br, snew = pl.pallas_call(
        _gla_sample_kernel, grid=(s,),
        in_specs=[pl.BlockSpec((1, 1, SEG_GLA), lambda b: (b, 0, 0)),
                  pl.BlockSpec((1, 1, GLA_HEADS, GLA_DK, GLA_DV), lambda b: (layer, b, 0, 0, 0)),
                  const2(128, 256), const2(1, 256), const2(1, GLA_DV)],
        out_specs=[pl.BlockSpec((1, 1, 512), lambda b: (b, 0, 0)),
                   pl.BlockSpec((1, GLA_HEADS, GLA_DK, GLA_DV), lambda b: (b, 0, 0, 0))],
        out_shape=[jax.ShapeDtypeStruct((s, 1, 512), BF16),
                   jax.ShapeDtypeStruct((s, GLA_HEADS, GLA_DK, GLA_DV), F32)],
        compiler_params=_cparams("parallel"), name="gla_sample",
    )(h.reshape(s, 1, SEG_GLA), state, *_gla_weights(p))
    return br.reshape(s, 512), snew


def _mla_prep_kernel(h_ref, cq_g_ref, wuq_ref, grp_ref, qg_ref, wuk_ref, ckv_g_ref, kr_g_ref,
                     cosq_ref, sinq_ref, cosk_ref, sink_ref, qcat_ref, lat_ref):
    h = h_ref[0]
    cq = h[:, 0:384]
    cqn = cq * lax.rsqrt(jnp.mean(cq * cq, axis=-1, keepdims=True) + NORM_EPS) * cq_g_ref[...]
    q = _dot(cqn, wuq_ref[...])
    ms = _dot_x3(q * q, grp_ref[...]) * (1.0 / (MLA_NOPE + MLA_ROPE))
    qn = q * lax.rsqrt(ms + NORM_EPS) * qg_ref[...]
    q_rope = _rope64(qn[:, 512:768], cosq_ref[...], sinq_ref[...])
    for hd in range(MLA_HEADS):
        q_lat = _dot(qn[:, MLA_NOPE * hd:MLA_NOPE * (hd + 1)], wuk_ref[hd])
        qcat_ref[0, hd] = jnp.concatenate(
            [q_lat.T, q_rope[:, MLA_ROPE * hd:MLA_ROPE * (hd + 1)].T], axis=0).astype(BF16)
    ckv = h[:, 384:512]
    c = ckv * lax.rsqrt(jnp.mean(ckv * ckv, axis=-1, keepdims=True) + NORM_EPS) * ckv_g_ref[...]
    kr = h[:, 512:640]
    ms_kr = jnp.sum(kr * kr, axis=-1, keepdims=True) * (1.0 / MLA_ROPE)
    krn = kr * lax.rsqrt(ms_kr + NORM_EPS) * kr_g_ref[...]
    krr = _rope64(krn, cosk_ref[...], sink_ref[...])
    lat_ref[0] = jnp.concatenate([c, krr[:, 0:MLA_ROPE]], axis=1)


def _mla_weights(p):
    hw = MLA_NOPE + MLA_ROPE
    w = p["mla_w_uq"].reshape(MLA_Q_LORA, MLA_HEADS, hw)
    wuq = jnp.concatenate([w[:, :, :MLA_NOPE].reshape(MLA_Q_LORA, -1),
                           w[:, :, MLA_NOPE:].reshape(MLA_Q_LORA, -1)], axis=1).astype(BF16)
    qg = jnp.concatenate([jnp.tile(p["mla_q_g"][:MLA_NOPE], MLA_HEADS),
                          jnp.tile(p["mla_q_g"][MLA_NOPE:], MLA_HEADS)]).reshape(1, -1)
    head = np.concatenate([np.arange(MLA_HEADS * MLA_NOPE) // MLA_NOPE, np.arange(MLA_HEADS * MLA_ROPE) // MLA_ROPE])
    grp = jnp.asarray((head[:, None] == head[None, :]).astype(np.float32), dtype=BF16)
    wuk_t = jnp.swapaxes(p["mla_w_uk"], 1, 2).astype(BF16)
    kr_g = jnp.concatenate([p["mla_kr_g"], jnp.zeros((128 - MLA_ROPE,), F32)]).reshape(1, 128)
    return (p["mla_cq_g"].reshape(1, -1), wuq, grp, qg, wuk_t, p["mla_ckv_g"].reshape(1, -1), kr_g)


def _mla_prep(h, tabs, p, tq):
    b, t, _ = h.shape
    tab = lambda w: pl.BlockSpec((tq, w), lambda bi, i: (i, 0))
    const2 = lambda r, w: pl.BlockSpec((r, w), lambda bi, i: (0, 0))
    return pl.pallas_call(
        _mla_prep_kernel, grid=(b, t // tq),
        in_specs=[pl.BlockSpec((1, tq, SEG_MLA), lambda bi, i: (bi, i, 0)),
                  const2(1, 384), const2(384, 768), const2(768, 768), const2(1, 768),
                  pl.BlockSpec((MLA_HEADS, MLA_NOPE, MLA_KV_LORA), lambda bi, i: (0, 0, 0)),
                  const2(1, 128), const2(1, 128), tab(256), tab(256), tab(128), tab(128)],
        out_specs=[pl.BlockSpec((1, MLA_HEADS, MLA_ROW, tq), lambda bi, i: (bi, 0, 0, i)),
                   pl.BlockSpec((1, tq, MLA_ROW), lambda bi, i: (bi, i, 0))],
        out_shape=[jax.ShapeDtypeStruct((b, MLA_HEADS, MLA_ROW, t), BF16),
                   jax.ShapeDtypeStruct((b, t, MLA_ROW), F32)],
        compiler_params=_cparams("parallel", "parallel"), name="mla_prep")(h, *_mla_weights(p), *tabs)


def _mla_up(o_lat, wuv_ref, z):
    outs = [_dot(o_lat[hd], wuv_ref[hd]) for hd in range(MLA_HEADS)]
    return (jnp.concatenate(outs, axis=1) * _silu(z)).astype(BF16)


def _mla_attn_kernel(q_ref, lat_ref, wuv_ref, z_ref, out_ref, m_s, l_s, acc_s, *, tq):
    i = pl.program_id(1)
    q_t = jnp.concatenate([q_ref[0, hd] for hd in range(MLA_HEADS)], axis=1)
    stats = (m_s, l_s, acc_s)
    _flash_reset(*stats)

    def tile(j, bias):
        start = pl.multiple_of(j * tq, tq)
        lat = lat_ref[0, pl.ds(start, tq), :].astype(BF16)
        s_t = _dot(lat, q_t) * MLA_SCALE
        if bias is not None:
            s_t = s_t + jnp.concatenate([bias] * MLA_HEADS, axis=1)
        _flash_update(s_t, lat[:, 0:MLA_KV_LORA], *stats)

    def full_tile(j, carry):
        tile(j, None)
        return carry

    lax.fori_loop(0, i, full_tile, 0)
    causal = (lax.broadcasted_iota(jnp.int32, (tq, tq), 0) <= lax.broadcasted_iota(jnp.int32, (tq, tq), 1))
    tile(i, jnp.where(causal, 0.0, NEG))
    o_t = acc_s[...] / l_s[...]
    out_ref[0] = _mla_up([o_t[:, tq * hd:tq * (hd + 1)].T for hd in range(MLA_HEADS)], wuv_ref, z_ref[0])


def _mla_attn(qcat_t, lat, w_uv, h, tq):
    b, _, _, t = qcat_t.shape
    cols = MLA_HEADS * tq
    return pl.pallas_call(
        functools.partial(_mla_attn_kernel, tq=tq), grid=(b, t // tq),
        in_specs=[pl.BlockSpec((1, MLA_HEADS, MLA_ROW, tq), lambda bi, i: (bi, 0, 0, i)),
                  pl.BlockSpec((1, t, MLA_ROW), lambda bi, i: (bi, 0, 0)),
                  pl.BlockSpec((MLA_HEADS, MLA_KV_LORA, MLA_VHD), lambda bi, i: (0, 0, 0)),
                  pl.BlockSpec((1, tq, 512), lambda bi, i: (bi, i, 0))],
        out_specs=pl.BlockSpec((1, tq, 512), lambda bi, i: (bi, i, 0)),
        out_shape=jax.ShapeDtypeStruct((b, t, 512), BF16),
        scratch_shapes=[pltpu.VMEM((1, cols), F32), pltpu.VMEM((1, cols), F32),
                        pltpu.VMEM((MLA_KV_LORA, cols), F32)],
        compiler_params=_cparams("parallel", "arbitrary"), name="mla_attn")(qcat_t, lat, w_uv.astype(BF16), h)


def _mla_sample_kernel(pt_ref, *refs, n_in):
    page_refs = refs[:n_in]
    q_ref, new_ref, o_ref, m_s, l_s, acc_s = refs[n_in:]
    c = pl.program_id(1)

    @pl.when(c == 0)
    def _():
        m_s[...] = jnp.full_like(m_s, NEG)
        l_s[...] = jnp.zeros_like(l_s)
        acc_s[...] = jnp.zeros_like(acc_s)

    q = q_ref[0]

    pages = [r[0, 0].astype(BF16) for r in page_refs]
    s = jnp.concatenate([_dot(q, pg) for pg in pages], axis=1) * MLA_SCALE
    m_new = jnp.maximum(m_s[...], jnp.max(s, axis=-1, keepdims=True))
    alpha = jnp.exp(m_s[...] - m_new)
    p = jnp.exp(s - m_new)
    l_s[...] = alpha * l_s[...] + jnp.sum(p, axis=-1, keepdims=True)
    acc = alpha * acc_s[...]
    for k, pg in enumerate(pages):
        acc = acc + _dot_nt(p[:, PAGE_SIZE * k:PAGE_SIZE * (k + 1)], pg[0:MLA_KV_LORA, :])
    acc_s[...] = acc
    m_s[...] = m_new

    @pl.when(c == pl.num_programs(1) - 1)
    def _():
        new = new_ref[0]
        s_new = jnp.sum(q.astype(F32) * new, axis=-1, keepdims=True) * MLA_SCALE
        m_fin = jnp.maximum(m_s[...], s_new)
        a_fin = jnp.exp(m_s[...] - m_fin)
        p_new = jnp.exp(s_new - m_fin)
        o_ref[0] = (a_fin * acc_s[...] + p_new * new[:, 0:MLA_KV_LORA]) / (a_fin * l_s[...] + p_new)


def _mla_sample_attn(cache_mla_t, layer, page_table, q, new_rows, pages_per_step):
    s, n_pages = page_table.shape
    n_in = pages_per_step

    def page_spec(k):
        return pl.BlockSpec((1, 1, MLA_ROW, PAGE_SIZE), lambda b, c, pt: (layer, pt[b, c * n_in + k], 0, 0))

    grid_spec = pltpu.PrefetchScalarGridSpec(
        num_scalar_prefetch=1, grid=(s, n_pages // n_in),
        in_specs=[page_spec(k) for k in range(n_in)]
        + [pl.BlockSpec((1, MLA_HEADS, MLA_ROW), lambda b, c, pt: (b, 0, 0)),
           pl.BlockSpec((1, 1, MLA_ROW), lambda b, c, pt: (b, 0, 0))],
        out_specs=pl.BlockSpec((1, MLA_HEADS, MLA_KV_LORA), lambda b, c, pt: (b, 0, 0)),
        scratch_shapes=[pltpu.VMEM((MLA_HEADS, 1), F32), pltpu.VMEM((MLA_HEADS, 1), F32),
                        pltpu.VMEM((MLA_HEADS, MLA_KV_LORA), F32)])
    return pl.pallas_call(
        functools.partial(_mla_sample_kernel, n_in=n_in), grid_spec=grid_spec,
        out_shape=jax.ShapeDtypeStruct((s, MLA_HEADS, MLA_KV_LORA), F32),
        compiler_params=_cparams("parallel", "arbitrary"), name="mla_sample_attn",
    )(page_table, *([cache_mla_t] * n_in), q, new_rows)


def _mla_post_sample_kernel(o_ref, wuv_ref, z_ref, out_ref):
    out_ref[...] = _mla_up([o_ref[:, hd, :] for hd in range(MLA_HEADS)], wuv_ref, z_ref[...])


def _mla_post_sample(o_lat, w_uv, z):
    s = o_lat.shape[0]
    return pl.pallas_call(
        _mla_post_sample_kernel, out_shape=jax.ShapeDtypeStruct((s, 512), BF16), name="mla_post_sample",
    )(o_lat, w_uv.astype(BF16), z)


def _pad_cols(w, width):
    return jnp.pad(w, ((0, 0), (0, width - w.shape[1])))


def _input_weights(w_in):
    sizes = (512, 384, 24, 512, 512, 512, 256, 256, 512, 16, 512, 384, 128, 64, 512)
    offs = np.concatenate([[0], np.cumsum(sizes)])
    seg = [w_in[:, offs[i]:offs[i + 1]] for i in range(len(sizes))]
    (nsa_q, nsa_kv, nsa_g, nsa_z, lru_x, lru_z, gla_q, gla_k, gla_v, gla_a, gla_z,
     mla_cq, mla_ckv, mla_kr, mla_z) = seg
    w_nsa = jnp.concatenate([nsa_q, nsa_kv, _pad_cols(nsa_g, 128), nsa_z], axis=1)
    w_lru = jnp.concatenate([lru_x, lru_z], axis=1)
    w_gla = jnp.concatenate([gla_q, gla_k, gla_v, _pad_cols(gla_a, 128), gla_z], axis=1)
    w_mla = jnp.concatenate([mla_cq, mla_ckv, _pad_cols(mla_kr, 128), mla_z], axis=1)
    w_merge = w_in[:, offs[-1]:]
    return tuple(w.astype(BF16) for w in (w_nsa, w_lru, w_gla, w_mla, w_merge))


def _layer_params(params, layer):
    return {k: v[layer] for k, v in params.items()}


def _prompt_layer(x, p, tabs):
    b, t, d = x.shape
    n = b * t
    w_nsa, w_lru, w_gla, w_mla, w_merge = _input_weights(p["w_in"])
    xn = _rmsnorm_bf16(x.reshape(n, d), p["norm_g"], 512)
    seg = lambda w, name: _matmul(xn, w, 512, name).reshape(b, t, -1)
    h_nsa, h_lru, h_gla, h_mla = (seg(w_nsa, "in_nsa"), seg(w_lru, "in_lru"), seg(w_gla, "in_gla"),
                                  seg(w_mla, "in_mla"))
    nsa_tabs, mla_tabs = tabs
    tq = 128
    qn, qr, rows, win, gate = _nsa_prep(h_nsa, nsa_tabs, p["nsa_q_g"], p["nsa_k_g"], tq)
    comp = _compress(rows, p["nsa_cmp_w"])
    o_cmp, sel = _nsa_cmp(qn, comp, tq)
    br_nsa = _nsa_attn(qr, sel, rows, win, o_cmp, gate, h_nsa, tq)
    br_lru, h_last, conv_new = _lru_prompt(h_lru, p, 256)
    br_gla, gla_state = _gla_prompt(h_gla, p)
    qcat, lat = _mla_prep(h_mla, mla_tabs, p, tq)
    br_mla = _mla_attn(qcat, lat, p["mla_w_uv"], h_mla[..., 640:1152], 256)
    flat = lambda a: a.reshape(n, BRANCH_W)
    merged = _merge(xn, [flat(br_nsa), flat(br_lru), flat(br_gla), flat(br_mla)], w_merge,
                    p["w_branch"].astype(BF16), 256)
    y = _outproj(merged, p["w_out"].astype(BF16), x.reshape(n, d), 256).reshape(b, t, d)
    keep = min(WINDOW, t)
    states = (rows.reshape(b, t, 4, NSA_HD), win[:, t - keep:].reshape(b, keep, 2, NSA_HD),
              h_last.reshape(b, BRANCH_W), conv_new, gla_state, lat)
    return y, states


def _sample_layer(x, p, layer, caches, tabs):
    s, t, d = x.shape
    cache_nsa_kv, cache_nsa_win, state_lru_h, state_lru_conv, state_gla, cache_mla, page_table = caches
    w_nsa, w_lru, w_gla, w_mla, w_merge = _input_weights(p["w_in"])
    x2 = x.reshape(s, d)
    xn = _rmsnorm_bf16(x2, p["norm_g"], s)
    h_nsa = _matmul(xn, w_nsa, s, "in_nsa_s")
    h_lru = _matmul(xn, w_lru, s, "in_lru_s")
    h_gla = _matmul(xn, w_gla, s, "in_gla_s")
    h_mla = _matmul(xn, w_mla, s, "in_mla_s")
    nsa_tabs, mla_tabs = tabs
    qn, qr, rows, win, gate = _nsa_prep(h_nsa[None], nsa_tabs, p["nsa_q_g"], p["nsa_k_g"], s)
    heads = lambda a: a.reshape(s, NSA_HEADS, NSA_HD)
    pages_per_step = min(SAMPLE_PAGES_PER_STEP, page_table.shape[1])
    o_cmp, sel = _nsa_cmp_sample(cache_nsa_kv, layer, page_table, heads(qn[0]), p["nsa_cmp_w"], pages_per_step)
    kvr = jnp.concatenate([rows[0], win[0]], axis=1).reshape(s, 1, 384)
    o_slc, o_win, new_win_t = _nsa_slc_sample(cache_nsa_kv, cache_nsa_win, layer, page_table,
                                              sel.reshape(s, 128), heads(qr[0]), kvr)
    flat = lambda a: a.reshape(s, BRANCH_W)
    br_nsa = _nsa_post_sample(flat(o_cmp), flat(o_slc), flat(o_win), gate[0], h_nsa[:, 1024:1536])
    br_lru, h_new = _lru_sample(h_lru, state_lru_conv[layer], state_lru_h[layer], p)
    conv_new = jnp.swapaxes(jnp.stack([state_lru_conv[layer, 1], state_lru_conv[layer, 2], h_lru[:, 0:512]]), 0, 1)
    br_gla, gla_state = _gla_sample(h_gla, state_gla, layer, p)
    qcat_t, lat = _mla_prep(h_mla[None], mla_tabs, p, s)
    o_lat = _mla_sample_attn(cache_mla, layer, page_table, jnp.transpose(qcat_t[0], (2, 0, 1)),
                             lat[0].reshape(s, 1, MLA_ROW), pages_per_step)
    br_mla = _mla_post_sample(o_lat, p["mla_w_uv"], h_mla[:, 640:1152])
    merged = _merge(xn, [br_nsa, br_lru, br_gla, br_mla], w_merge, p["w_branch"].astype(BF16), s)
    y = _outproj(merged, p["w_out"].astype(BF16), x2, s).reshape(s, t, d)
    states = (rows[0].reshape(s, 1, 4, NSA_HD), jnp.transpose(new_win_t[0], (0, 3, 1, 2)), h_new, conv_new,
              gla_state, lat[0].reshape(s, 1, MLA_ROW))
    return y, states


def kernel(x_prompt, x_sample, cache_nsa_kv, cache_nsa_win, state_lru_h, state_lru_conv, state_gla, cache_mla,
           page_table, norm_g, w_in, nsa_q_g, nsa_k_g, nsa_cmp_w, lru_conv_w, lru_conv_b, lru_wa, lru_ba, lru_wx,
           lru_bx, lru_lam, gla_w_a2, gla_b_a, gla_out_g, mla_cq_g, mla_w_uq, mla_q_g, mla_ckv_g, mla_kr_g,
           mla_w_uk, mla_w_uv, w_branch, w_out):
    params = dict(norm_g=norm_g, w_in=w_in, nsa_q_g=nsa_q_g, nsa_k_g=nsa_k_g, nsa_cmp_w=nsa_cmp_w,
                  lru_conv_w=lru_conv_w, lru_conv_b=lru_conv_b, lru_wa=lru_wa, lru_ba=lru_ba, lru_wx=lru_wx,
                  lru_bx=lru_bx, lru_lam=lru_lam, gla_w_a2=gla_w_a2, gla_b_a=gla_b_a, gla_out_g=gla_out_g,
                  mla_cq_g=mla_cq_g, mla_w_uq=mla_w_uq, mla_q_g=mla_q_g, mla_ckv_g=mla_ckv_g,
                  mla_kr_g=mla_kr_g, mla_w_uk=mla_w_uk, mla_w_uv=mla_w_uv, w_branch=w_branch, w_out=w_out)
    depth = w_in.shape[0]
    assert x_prompt.shape[2] == N_BRANCH * BRANCH_W and x_sample.shape[1] == 1
    t = x_prompt.shape[1]
    n_seq, n_pages = page_table.shape
    past_len = n_pages * PAGE_SIZE
    assert t % 256 == 0 and past_len % SLC_BLOCK == 0 and cache_nsa_win.shape[2] == WINDOW
    caches = (jnp.transpose(cache_nsa_kv, (0, 1, 3, 4, 2)), jnp.transpose(cache_nsa_win, (0, 1, 3, 4, 2)),
              state_lru_h, jnp.transpose(state_lru_conv, (0, 2, 1, 3)), state_gla,
              jnp.transpose(cache_mla, (0, 1, 3, 2)), page_table)
    tabs_p = _rope_tables(jnp.arange(t))
    tabs_s = _rope_tables(jnp.full((n_seq,), past_len))
    y_p, y_s = x_prompt, x_sample
    sp, ss = [], []
    for layer in range(depth):
        p = _layer_params(params, layer)
        y_p, st = _prompt_layer(y_p, p, tabs_p)
        sp.append(st)
        y_s, st = _sample_layer(y_s, p, layer, caches, tabs_s)
        ss.append(st)
    stack = lambda lst, i: jnp.stack([s[i] for s in lst])
    return (y_p, y_s, stack(sp, 0), stack(ss, 0), stack(sp, 1), stack(ss, 1), stack(sp, 2), stack(ss, 2),
            stack(sp, 3), stack(ss, 3), stack(sp, 4), stack(ss, 4), stack(sp, 5), stack(ss, 5))
```

```python
import functools

import numpy as np
import jax
import jax.numpy as jnp
from jax import lax
from jax.experimental import pallas as pl
from jax.experimental.pallas import tpu as pltpu

F32 = jnp.float32
BF16 = jnp.bfloat16
NEG = -1e30

N_BRANCH = 4
NSA_HEADS = 8
NSA_HD = 64
CMP_BLOCK = 32
CMP_STRIDE = 16
SLC_BLOCK = 64
SLC_TOPN = 16
WINDOW = 512
PAGE_SIZE = 128
LRU_BLOCKS = 8
CONV_W = 4
LRU_C = 8.0
GLA_HEADS = 4
GLA_DK = 64
GLA_DV = 128
GLA_RANK = 16
GLA_TAU = 16.0
GLA_CHUNK = 128
MLA_HEADS = 4
MLA_NOPE = 128
MLA_ROPE = 64
MLA_VHD = 128
MLA_Q_LORA = 384
MLA_KV_LORA = 128
MLA_ROW = MLA_KV_LORA + MLA_ROPE
ROPE_THETA = 10000.0
NORM_EPS = 1e-6

BRANCH_W = 512
NSA_SCALE = NSA_HD ** -0.5
MLA_SCALE = (MLA_NOPE + MLA_ROPE) ** -0.5

SEG_NSA = 1536
SEG_LRU = 1024
SEG_GLA = 1664
SEG_MLA = 1152

VMEM_LIMIT = 48 * 1024 * 1024
STREAM_GROUP_PAGES = 16
STREAM_SLOTS = 3


def _cparams(*sem):
    return pltpu.CompilerParams(dimension_semantics=sem, vmem_limit_bytes=VMEM_LIMIT)


def _dot(a, b):
    return jnp.dot(a.astype(BF16), b.astype(BF16), preferred_element_type=F32)


def _dot_nt(a, b):
    return lax.dot_general(a.astype(BF16), b.astype(BF16), (((1,), (1,)), ((), ())),
                           preferred_element_type=F32)


def _dot_tn(a, b):
    return lax.dot_general(a.astype(BF16), b.astype(BF16), (((0,), (0,)), ((), ())),
                           preferred_element_type=F32)


def _split3(a):
    a0 = a.astype(BF16)
    r = a - a0.astype(F32)
    a1 = r.astype(BF16)
    a2 = (r - a1.astype(F32)).astype(BF16)
    return a0, a1, a2


def _dot_x3(a, b_exact):
    a0, a1, a2 = _split3(a)
    d = functools.partial(jnp.dot, preferred_element_type=F32)
    return d(a0, b_exact) + d(a1, b_exact) + d(a2, b_exact)


def _x3_dot(a_exact, b):
    b0, b1, b2 = _split3(b)
    d = functools.partial(jnp.dot, preferred_element_type=F32)
    return d(a_exact, b0) + d(a_exact, b1) + d(a_exact, b2)


def _rope64(x, cos, sin_signed):
    w = x.shape[1]
    lane = lax.broadcasted_iota(jnp.int32, x.shape, 1)
    swapped = jnp.where((lane & 63) < 32, pltpu.roll(x, w - 32, 1), pltpu.roll(x, 32, 1))
    return x * cos + swapped * sin_signed


def _heads_to_rows(x, n_heads, hd):
    return jnp.concatenate([x[:, hd * h:hd * (h + 1)] for h in range(n_heads)], axis=0)


def _rows_to_heads(x, n_heads):
    t = x.shape[0] // n_heads
    return jnp.concatenate([x[t * h:t * (h + 1)] for h in range(n_heads)], axis=1)


def _silu(z):
    return z * jax.nn.sigmoid(z)


def _norm_kernel(x_ref, g_ref, o_ref):
    x = x_ref[...]
    ms = jnp.mean(x * x, axis=-1, keepdims=True)
    o_ref[...] = (x * lax.rsqrt(ms + NORM_EPS) * g_ref[...]).astype(o_ref.dtype)


def _rmsnorm_bf16(x, g, tm):
    n, d = x.shape
    return pl.pallas_call(
        _norm_kernel, grid=(n // tm,),
        in_specs=[pl.BlockSpec((tm, d), lambda i: (i, 0)), pl.BlockSpec((1, d), lambda i: (0, 0))],
        out_specs=pl.BlockSpec((tm, d), lambda i: (i, 0)),
        out_shape=jax.ShapeDtypeStruct((n, d), BF16),
        compiler_params=_cparams("parallel"), name="rmsnorm")(x, g.reshape(1, d))


def _mm_kernel(x_ref, w_ref, o_ref):
    o_ref[...] = jnp.dot(x_ref[...], w_ref[...], preferred_element_type=F32)


def _matmul(x, w, tm, name):
    n, k = x.shape
    c = w.shape[1]
    return pl.pallas_call(
        _mm_kernel, grid=(n // tm,),
        in_specs=[pl.BlockSpec((tm, k), lambda i: (i, 0)), pl.BlockSpec((k, c), lambda i: (0, 0))],
        out_specs=pl.BlockSpec((tm, c), lambda i: (i, 0)),
        out_shape=jax.ShapeDtypeStruct((n, c), F32),
        compiler_params=_cparams("parallel"), name=name)(x, w)


def _merge_kernel(xn_ref, b0_ref, b1_ref, b2_ref, b3_ref, wm_ref, wb_ref, o_ref, acc_ref):
    n = pl.program_id(1)

    @pl.when(n == 0)
    def _():
        acc_ref[...] = jnp.zeros_like(acc_ref)

    gate = jax.nn.sigmoid(jnp.dot(xn_ref[...], wm_ref[...], preferred_element_type=F32))
    for k, b_ref in enumerate((b0_ref, b1_ref, b2_ref, b3_ref)):
        @pl.when(n == k)
        def _(b_ref=b_ref):
            proj = jnp.dot(b_ref[...], wb_ref[0], preferred_element_type=F32)
            acc_ref[...] += gate * proj

    @pl.when(n == N_BRANCH - 1)
    def _():
        o_ref[...] = acc_ref[...].astype(o_ref.dtype)


def _merge(xn, branches, w_merge, w_branch, tm):
    n, d = xn.shape
    bw = branches[0].shape[1]
    bspec = pl.BlockSpec((tm, bw), lambda i, j: (i, 0))
    return pl.pallas_call(
        _merge_kernel, grid=(n // tm, N_BRANCH),
        in_specs=[pl.BlockSpec((tm, d), lambda i, j: (i, 0)), bspec, bspec, bspec, bspec,
                  pl.BlockSpec((d, d), lambda i, j: (0, j)),
                  pl.BlockSpec((1, bw, d), lambda i, j: (j, 0, 0))],
        out_specs=pl.BlockSpec((tm, d), lambda i, j: (i, 0)),
        out_shape=jax.ShapeDtypeStruct((n, d), BF16),
        scratch_shapes=[pltpu.VMEM((tm, d), F32)],
        compiler_params=_cparams("parallel", "arbitrary"), name="merge")(xn, *branches, w_merge, w_branch)


def _outproj_kernel(m_ref, w_ref, x_ref, o_ref):
    o_ref[...] = x_ref[...] + jnp.dot(m_ref[...], w_ref[...], preferred_element_type=F32)


def _outproj(merged, w_out, x, tm):
    n, d = x.shape
    return pl.pallas_call(
        _outproj_kernel, grid=(n // tm,),
        in_specs=[pl.BlockSpec((tm, d), lambda i: (i, 0)), pl.BlockSpec((d, d), lambda i: (0, 0)),
                  pl.BlockSpec((tm, d), lambda i: (i, 0))],
        out_specs=pl.BlockSpec((tm, d), lambda i: (i, 0)),
        out_shape=jax.ShapeDtypeStruct((n, d), F32),
        compiler_params=_cparams("parallel"), name="outproj")(merged, w_out, x)


def _nsa_prep_kernel(h_ref, cq_ref, sq_ref, ckv_ref, skv_ref, qg_ref, kvg_ref, kvm_ref, bdq_ref, bdkv_ref,
                     qn_ref, qr_ref, rows_ref, win_ref, gate_ref):
    h = h_ref[0]
    q = h[:, 0:512]
    kv = h[:, 512:896]
    ms_q = _dot_x3(q * q, bdq_ref[...]) * (1.0 / NSA_HD)
    qn = q * lax.rsqrt(ms_q + NORM_EPS) * qg_ref[...]
    qn_ref[0] = qn
    qr_ref[0] = _rope64(qn, cq_ref[...], sq_ref[...])
    ms_kv = _dot_x3(kv * kv, bdkv_ref[...]) * (1.0 / NSA_HD)
    kvn = kv * lax.rsqrt(ms_kv + NORM_EPS) * kvg_ref[...]
    kvn = jnp.where(kvm_ref[...] > 0.5, kvn, kv)
    kvr = _rope64(kvn, ckv_ref[...], skv_ref[...])
    rows_ref[0] = kvr[:, 0:256]
    win_ref[0] = kvr[:, 256:384]
    gate_ref[0] = jax.nn.sigmoid(h[:, 896:1024])


def _block_diag_ones(width, group):
    idx = np.arange(width) // group
    return jnp.asarray((idx[:, None] == idx[None, :]).astype(np.float32), dtype=BF16)


def _nsa_prep(h, tabs, q_gain, k_gain, tq):
    b, t, _ = h.shape
    cq, sq, ckv, skv = tabs
    one = jnp.ones((NSA_HD,), F32)
    kvg = jnp.concatenate([k_gain[0], one, k_gain[1], one, k_gain[2], one]).reshape(1, 384)
    kvm = jnp.concatenate([one, 0 * one, one, 0 * one, one, 0 * one]).reshape(1, 384)
    qg = jnp.tile(q_gain, NSA_HEADS).reshape(1, 512)
    tile = lambda w: pl.BlockSpec((1, tq, w), lambda bi, i: (bi, i, 0))
    tab = lambda w: pl.BlockSpec((tq, w), lambda bi, i: (i, 0))
    const = lambda r, w: pl.BlockSpec((r, w), lambda bi, i: (0, 0))
    return pl.pallas_call(
        _nsa_prep_kernel, grid=(b, t // tq),
        in_specs=[tile(SEG_NSA), tab(512), tab(512), tab(384), tab(384), const(1, 512), const(1, 384),
                  const(1, 384), const(512, 512), const(384, 384)],
        out_specs=[tile(512), tile(512), tile(256), tile(128), tile(128)],
        out_shape=[jax.ShapeDtypeStruct((b, t, 512), F32), jax.ShapeDtypeStruct((b, t, 512), F32),
                   jax.ShapeDtypeStruct((b, t, 256), F32), jax.ShapeDtypeStruct((b, t, 128), F32),
                   jax.ShapeDtypeStruct((b, t, 128), F32)],
        compiler_params=_cparams("parallel", "parallel"), name="nsa_prep",
    )(h, cq, sq, ckv, skv, qg, kvg, kvm, _block_diag_ones(512, 64), _block_diag_ones(384, 64))


def _rope_base(pos):
    inv = ROPE_THETA ** (-jnp.arange(0, 64, 2, dtype=F32) / 64)
    ang = pos.astype(F32)[:, None] * inv[None, :]
    c = jnp.cos(ang)
    s = jnp.sin(ang)
    return jnp.concatenate([c, c], axis=-1), jnp.concatenate([-s, s], axis=-1)


def _rope_tables(pos):
    c, s = _rope_base(pos)
    one = jnp.ones_like(c)
    zero = jnp.zeros_like(s)
    nsa = (jnp.tile(c, (1, NSA_HEADS)), jnp.tile(s, (1, NSA_HEADS)),
           jnp.concatenate([one, one, c, one, c, one], axis=1),
           jnp.concatenate([zero, zero, s, zero, s, zero], axis=1))
    mla = (jnp.tile(c, (1, MLA_HEADS)), jnp.tile(s, (1, MLA_HEADS)),
           jnp.concatenate([c, one], axis=1), jnp.concatenate([s, zero], axis=1))
    return nsa, mla


def _compress_kernel(rows_ref, w1_ref, w2_ref, comp_ref):
    t = rows_ref.shape[1]
    n = t // CMP_STRIDE
    first = jnp.zeros((n, 128), F32)
    second = jnp.zeros((n, 128), F32)
    for j in range(CMP_STRIDE):
        r = rows_ref[0, pl.ds(j, n, stride=CMP_STRIDE), :]
        first = first + r * w1_ref[j:j + 1, :]
        second = second + r * w2_ref[j:j + 1, :]
    nxt = pltpu.roll(second, n - 1, 0)
    row = lax.broadcasted_iota(jnp.int32, (n, 128), 0)
    comp_ref[0] = jnp.where(row < n - 1, first + nxt, 0.0)


def _cmp_weights(w_pos):
    w1 = jnp.concatenate([w_pos[0, :CMP_STRIDE], w_pos[1, :CMP_STRIDE]], axis=1)
    w2 = jnp.concatenate([w_pos[0, CMP_STRIDE:], w_pos[1, CMP_STRIDE:]], axis=1)
    return w1, w2


def _compress(rows, w_pos):
    b, t, _ = rows.shape
    w1, w2 = _cmp_weights(w_pos)
    n = t // CMP_STRIDE
    return pl.pallas_call(
        _compress_kernel, grid=(b,),
        in_specs=[pl.BlockSpec((1, t, 128), lambda bi: (bi, 0, 0)),
                  pl.BlockSpec((CMP_STRIDE, 128), lambda bi: (0, 0)),
                  pl.BlockSpec((CMP_STRIDE, 128), lambda bi: (0, 0))],
        out_specs=pl.BlockSpec((1, n, 128), lambda bi: (bi, 0, 0)),
        out_shape=jax.ShapeDtypeStruct((b, n, 128), F32),
        compiler_params=_cparams("parallel"), name="nsa_compress")(rows, w1, w2)


def _score_matrix(n_cmp, n_slc):
    m = np.zeros((n_cmp, n_slc), np.float32)
    per = SLC_BLOCK // CMP_STRIDE
    for n in range(n_cmp):
        for s in (n, n + 1):
            if s // per < n_slc:
                m[n, s // per] += 1.0
    return jnp.asarray(m, dtype=BF16)


def _nsa_cmp_kernel(qn_ref, comp_ref, m_ref, o_ref, sel_ref, *, tq):
    i = pl.program_id(1)
    qh = _heads_to_rows(qn_ref[0], NSA_HEADS, NSA_HD)
    comp = comp_ref[0]
    nc = comp.shape[0]
    s = _dot_nt(qh, comp[:, 0:64]) * NSA_SCALE
    qpos = i * tq + lax.broadcasted_iota(jnp.int32, (tq, nc), 0)
    blk_end = lax.broadcasted_iota(jnp.int32, (tq, nc), 1) * CMP_STRIDE + (CMP_BLOCK - 1)
    mask = (blk_end <= qpos)[None]
    s3 = jnp.where(mask, s.reshape(NSA_HEADS, tq, nc), NEG)
    mx = jnp.max(s3, axis=-1, keepdims=True)
    e = jnp.where(mask, jnp.exp(s3 - mx), 0.0)
    p3 = e / jnp.maximum(jnp.sum(e, axis=-1, keepdims=True), 1e-30)
    o = _dot(p3.reshape(NSA_HEADS * tq, nc), comp[:, 64:128])
    o_ref[0] = _rows_to_heads(o, NSA_HEADS)
    score = _dot_x3(jnp.sum(p3, axis=0), m_ref[...])
    nsb = score.shape[1]
    blk = lax.broadcasted_iota(jnp.int32, (tq, nsb), 1)
    qp = i * tq + lax.broadcasted_iota(jnp.int32, (tq, nsb), 0)
    cur = jnp.right_shift(qp, 6)
    valid = blk * SLC_BLOCK <= qp
    forced = (blk == 0) | (blk == cur) | (blk == cur - 1)
    sc = jnp.where(valid, jnp.where(forced, 1e30, score), -1e30)
    rank = jnp.zeros((tq, nsb), F32)
    for b in range(nsb):
        col = sc[:, b:b + 1]
        rank = rank + jnp.where((col > sc) | ((col == sc) & (blk > b)), 1.0, 0.0)
    sel_ref[0] = jnp.where((rank < SLC_TOPN) & valid, 1.0, 0.0)


def _nsa_cmp(qn, comp, tq):
    b, t, _ = qn.shape
    nc = comp.shape[1]
    nsb = t // SLC_BLOCK
    return pl.pallas_call(
        functools.partial(_nsa_cmp_kernel, tq=tq), grid=(b, t // tq),
        in_specs=[pl.BlockSpec((1, tq, 512), lambda bi, i: (bi, i, 0)),
                  pl.BlockSpec((1, nc, 128), lambda bi, i: (bi, 0, 0)),
                  pl.BlockSpec((nc, nsb), lambda bi, i: (0, 0))],
        out_specs=[pl.BlockSpec((1, tq, 512), lambda bi, i: (bi, i, 0)),
                   pl.BlockSpec((1, tq, nsb), lambda bi, i: (bi, i, 0))],
        out_shape=[jax.ShapeDtypeStruct((b, t, 512), F32), jax.ShapeDtypeStruct((b, t, nsb), F32)],
        compiler_params=_cparams("parallel", "parallel"), name="nsa_cmp")(qn, comp, _score_matrix(nc, nsb))


def _gate_expand(c):
    g = np.zeros((128, 512), np.float32)
    for h in range(NSA_HEADS):
        g[3 * h + c, NSA_HD * h:NSA_HD * (h + 1)] = 1.0
    return g


def _gate_mats():
    return jnp.asarray(np.stack([_gate_expand(c) for c in range(3)]), dtype=BF16)


def _nsa_combine(o_cmp, o_slc, o_win, gate, gx_ref, z):
    g0 = _dot_x3(gate, gx_ref[0])
    g1 = _dot_x3(gate, gx_ref[1])
    g2 = _dot_x3(gate, gx_ref[2])
    return ((g0 * o_cmp + g1 * o_slc + g2 * o_win) * _silu(z)).astype(BF16)


def _flash_reset(m_s, l_s, acc_s):
    m_s[...] = jnp.full_like(m_s, NEG)
    l_s[...] = jnp.zeros_like(l_s)
    acc_s[...] = jnp.zeros_like(acc_s)


def _flash_update(s_t, v, m_s, l_s, acc_s):
    m_new = jnp.maximum(m_s[...], jnp.max(s_t, axis=0, keepdims=True))
    alpha = jnp.exp(m_s[...] - m_new)
    p = jnp.exp(s_t - m_new)
    l_s[...] = alpha * l_s[...] + jnp.sum(p, axis=0, keepdims=True)
    acc_s[...] = alpha * acc_s[...] + _dot_tn(v, p)
    m_s[...] = m_new


def _nsa_attn_kernel(qr_ref, sel_ref, rows_ref, win_ref, ocmp_ref, gate_ref, gx_ref, z_ref, out_ref,
                     m_s, l_s, acc_s, *, tq, tk):
    i = pl.program_id(1)
    q = qr_ref[0] * NSA_SCALE
    q_t = jnp.concatenate([q[:, NSA_HD * h:NSA_HD * (h + 1)].T for h in range(NSA_HEADS)], axis=1).astype(BF16)
    sel_t = sel_ref[0].T.astype(BF16)
    nsb = sel_t.shape[0]
    koff = lax.broadcasted_iota(jnp.int32, (tq, tq), 0)
    qpos = i * tq + lax.broadcasted_iota(jnp.int32, (tq, tq), 1)
    stats = (m_s, l_s, acc_s)

    def scores(k, mask):
        bias = jnp.where(mask, 0.0, NEG)
        return _dot(k, q_t) + jnp.concatenate([bias] * NSA_HEADS, axis=1)

    def finish():
        o_t = acc_s[...] / l_s[...]
        return jnp.concatenate([o_t[:, tq * h:tq * (h + 1)].T for h in range(NSA_HEADS)], axis=1)

    def slc_tile(j, carry):
        start = pl.multiple_of(j * tk, tk)
        k = rows_ref[0, pl.ds(start, tk), 128:192]
        v = rows_ref[0, pl.ds(start, tk), 192:256]
        kpos = start + lax.broadcasted_iota(jnp.int32, (tk, tq), 0)
        expand = (lax.broadcasted_iota(jnp.int32, (tk, nsb), 1)
                  == jnp.right_shift(start + lax.broadcasted_iota(jnp.int32, (tk, nsb), 0), 6))
        selk = jnp.dot(jnp.where(expand, 1.0, 0.0).astype(BF16), sel_t, preferred_element_type=F32)
        visible = kpos <= i * tq + lax.broadcasted_iota(jnp.int32, (tk, tq), 1)
        _flash_update(scores(k, visible & (selk > 0.5)), v, *stats)
        return carry

    _flash_reset(*stats)
    lax.fori_loop(0, (i * tq + tq + tk - 1) // tk, slc_tile, 0)
    o_slc = finish()

    def win_tile(n, carry):
        start = pl.multiple_of((i - n) * tq, tq)
        k = win_ref[0, pl.ds(start, tq), 0:64]
        v = win_ref[0, pl.ds(start, tq), 64:128]
        kpos = start + koff
        _flash_update(scores(k, (kpos <= qpos) & (kpos > qpos - WINDOW)), v, *stats)
        return carry

    _flash_reset(*stats)
    lax.fori_loop(0, jnp.minimum(i, WINDOW // tq) + 1, win_tile, 0)
    o_win = finish()
    out_ref[0] = _nsa_combine(ocmp_ref[0], o_slc, o_win, gate_ref[0], gx_ref, z_ref[0])


def _nsa_attn(qr, sel, rows, win, o_cmp, gate, h, tq):
    b, t, _ = qr.shape
    nsb = sel.shape[2]
    tile = lambda w: pl.BlockSpec((1, tq, w), lambda bi, i: (bi, i, 0))
    return pl.pallas_call(
        functools.partial(_nsa_attn_kernel, tq=tq, tk=4 * tq), grid=(b, t // tq),
        in_specs=[tile(512), tile(nsb),
                  pl.BlockSpec((1, t, 256), lambda bi, i: (bi, 0, 0)),
                  pl.BlockSpec((1, t, 128), lambda bi, i: (bi, 0, 0)),
                  tile(512), tile(128),
                  pl.BlockSpec((3, 128, 512), lambda bi, i: (0, 0, 0)),
                  pl.BlockSpec((1, tq, 512), lambda bi, i: (bi, i, 2))],
        out_specs=tile(512),
        out_shape=jax.ShapeDtypeStruct((b, t, 512), BF16),
        scratch_shapes=[pltpu.VMEM((1, NSA_HEADS * tq), F32), pltpu.VMEM((1, NSA_HEADS * tq), F32),
                        pltpu.VMEM((NSA_HD, NSA_HEADS * tq), F32)],
        compiler_params=_cparams("parallel", "arbitrary"), name="nsa_attn",
    )(qr, sel, rows, win, o_cmp, gate, _gate_mats(), h)


def _page_stream(pt_ref, buf, sem, n_groups, group, page_src, compute):
    slots = buf.shape[0]
    total = pt_ref.shape[0] * n_groups
    assert total >= slots

    def copies(it):
        b = it // n_groups
        g = it - b * n_groups
        slot = it % slots
        return [pltpu.make_async_copy(page_src(pt_ref[b, g * group + k]), buf.at[slot, k], sem.at[slot])
                for k in range(group)]

    for it0 in range(slots - 1):
        for cp in copies(it0):
            cp.start()

    def body(it, carry):
        for cp in copies(it):
            cp.wait()

        @pl.when(it + (slots - 1) < total)
        def _():
            for cp in copies(it + (slots - 1)):
                cp.start()

        b = it // n_groups
        compute(b, it - b * n_groups, it % slots)
        return carry

    lax.fori_loop(0, total, body, 0)


def _nsa_cmp_sample_kernel(pt_ref, cache_ref, qn_ref, wt_ref, g8_ref, m_ref, o_ref, sel_ref,
                           buf, sem, sums_s, *, layer, n_pages, past_len):
    per_page = PAGE_SIZE // CMP_STRIDE
    group = buf.shape[1]
    n_groups = n_pages // group
    nc = n_pages * per_page
    sums_s[pl.ds(nc, 8), :] = jnp.zeros((8, 4 * NSA_HD), F32)

    def compute(b, g, slot):
        base = g * (group * per_page)
        for k in range(group):
            k_t = buf[slot, k, 0]
            v_t = buf[slot, k, 1]
            stacked = jnp.concatenate([(k_t * wt_ref[0]).astype(BF16), (k_t * wt_ref[1]).astype(BF16),
                                       (v_t * wt_ref[2]).astype(BF16), (v_t * wt_ref[3]).astype(BF16)], axis=0)
            off = pl.multiple_of(base + k * per_page, per_page)
            sums_s[pl.ds(off, per_page), :] = lax.dot_general(
                g8_ref[...], stacked, (((1,), (1,)), ((), ())), preferred_element_type=F32)

        @pl.when(g == n_groups - 1)
        def _():
            _nsa_cmp_sample_finish(b, qn_ref, m_ref, o_ref, sel_ref, sums_s, nc=nc, past_len=past_len)

    _page_stream(pt_ref, buf, sem, n_groups, group, lambda page: cache_ref.at[layer, page, pl.ds(0, 2)], compute)


def _nsa_cmp_sample_finish(b, qn_ref, m_ref, o_ref, sel_ref, sums_s, *, nc, past_len):
    first = sums_s[pl.ds(0, nc), :]
    second = sums_s[pl.ds(1, nc), :]
    ck = first[:, 0:64] + second[:, 64:128]
    cv = first[:, 128:192] + second[:, 192:256]
    q = qn_ref[b]
    s = _dot_nt(q, ck) * NSA_SCALE
    blk_end = lax.broadcasted_iota(jnp.int32, (NSA_HEADS, nc), 1) * CMP_STRIDE + (CMP_BLOCK - 1)
    mask = blk_end <= past_len
    s = jnp.where(mask, s, NEG)
    e = jnp.where(mask, jnp.exp(s - jnp.max(s, axis=-1, keepdims=True)), 0.0)
    p = e / jnp.maximum(jnp.sum(e, axis=-1, keepdims=True), 1e-30)
    o_ref[b] = _dot(p, cv)
    imp = jnp.sum(p, axis=0, keepdims=True)
    score = _dot_x3(imp, m_ref[...])
    nsb = score.shape[1]
    lane = lax.broadcasted_iota(jnp.int32, (nsb, nsb), 1)
    sub = lax.broadcasted_iota(jnp.int32, (nsb, nsb), 0)
    forced_row = (lane == 0) | (lane == nsb - 1)
    sc_row = jnp.where(forced_row, 1e30, jnp.broadcast_to(score, (nsb, nsb)))
    eye = lane == sub
    sc_col = jnp.sum(jnp.where(eye, sc_row, 0.0), axis=1, keepdims=True)
    beats_row = (sc_col > sc_row) | ((sc_col == sc_row) & (sub < lane))
    rank_row = jnp.sum(jnp.where(beats_row, 1.0, 0.0), axis=0, keepdims=True)
    sel_row = rank_row < (SLC_TOPN - 1)
    beats_col = (sc_row > sc_col) | ((sc_row == sc_col) & (lane < sub))
    rank_col = jnp.sum(jnp.where(beats_col, 1.0, 0.0), axis=1, keepdims=True)
    sel_col = rank_col < (SLC_TOPN - 1)
    pos_col = jnp.sum(jnp.where(sel_row & (lane < sub), 1.0, 0.0), axis=1, keepdims=True)
    slot = lax.broadcasted_iota(jnp.int32, (nsb, 128), 1).astype(F32)
    blk_id = lax.broadcasted_iota(jnp.int32, (nsb, 128), 0).astype(F32)
    onehot = sel_col & (pos_col == slot)
    sel_ref[b] = jnp.sum(jnp.where(onehot, blk_id, 0.0), axis=0, keepdims=True).astype(jnp.int32)


def _nsa_cmp_sample(cache_kv_t, layer, page_table, qn, w_pos, pages_per_step):
    s, n_pages = page_table.shape
    past_len = n_pages * PAGE_SIZE
    per_page = PAGE_SIZE // CMP_STRIDE
    nc = n_pages * per_page
    nsb = past_len // SLC_BLOCK
    group = pages_per_step
    assert n_pages % group == 0
    tile_t = lambda w: jnp.tile(w.T, (1, per_page))
    wt = jnp.stack([tile_t(w_pos[0, :CMP_STRIDE]), tile_t(w_pos[0, CMP_STRIDE:]),
                    tile_t(w_pos[1, :CMP_STRIDE]), tile_t(w_pos[1, CMP_STRIDE:])])
    g8 = jnp.asarray((np.arange(per_page)[:, None] == (np.arange(PAGE_SIZE) // CMP_STRIDE)[None, :])
                     .astype(np.float32), dtype=BF16)
    vmem = pl.BlockSpec(memory_space=pltpu.VMEM)
    return pl.pallas_call(
        functools.partial(_nsa_cmp_sample_kernel, layer=layer, n_pages=n_pages, past_len=past_len),
        in_specs=[pl.BlockSpec(memory_space=pltpu.SMEM), pl.BlockSpec(memory_space=pl.ANY),
                  vmem, vmem, vmem, vmem],
        out_specs=[vmem, vmem],
        out_shape=[jax.ShapeDtypeStruct((s, NSA_HEADS, NSA_HD), F32),
                   jax.ShapeDtypeStruct((s, 1, 128), jnp.int32)],
        scratch_shapes=[pltpu.VMEM((STREAM_SLOTS, group, 2, NSA_HD, PAGE_SIZE), F32),
                        pltpu.SemaphoreType.DMA((STREAM_SLOTS,)),
                        pltpu.VMEM((nc + 8, 4 * NSA_HD), F32)],
        compiler_params=pltpu.CompilerParams(vmem_limit_bytes=VMEM_LIMIT), name="nsa_cmp_sample",
    )(page_table, cache_kv_t, qn, wt, g8, _score_matrix(nc, nsb))


def _attend_with_new(q, s_past, v_t_parts, k_new, v_new):
    s_new = jnp.sum(q * k_new, axis=-1, keepdims=True)
    m = jnp.maximum(jnp.max(s_past, axis=-1, keepdims=True), s_new)
    p = jnp.exp(s_past - m)
    p_new = jnp.exp(s_new - m)
    acc = p_new * v_new
    off = 0
    for v_t in v_t_parts:
        n = v_t.shape[1]
        acc = acc + _dot_nt(p[:, off:off + n], v_t)
        off += n
    return acc / (jnp.sum(p, axis=-1, keepdims=True) + p_new)


def _nsa_slc_sample_kernel(sel_ref, pt_ref, *refs, n_sel, past_len):
    b = pl.program_id(0)
    blk_refs = refs[:n_sel]
    qr_ref, new_ref, winbuf_ref, oslc_ref, owin_ref, newwin_ref = refs[n_sel:]
    q = qr_ref[0] * NSA_SCALE
    new = new_ref[0]
    per_page = PAGE_SIZE // SLC_BLOCK
    lane = lax.broadcasted_iota(jnp.int32, (NSA_HEADS, PAGE_SIZE), 1)
    parts = []
    for k, r in enumerate(blk_refs):
        half = sel_ref[b, k] % per_page
        parts.append(jnp.where(jnp.right_shift(lane, 6) == half, _dot(q, r[0, 0, 0]), NEG))
    oslc_ref[0] = _attend_with_new(q, jnp.concatenate(parts, axis=1), [r[0, 0, 1] for r in blk_refs],
                                   new[:, 128:192], new[:, 192:256])
    lw = winbuf_ref.shape[4]
    kw_t = winbuf_ref[0, 0, 0]
    vw_t = winbuf_ref[0, 0, 1]
    k_pos = past_len - lw + lax.broadcasted_iota(jnp.int32, (NSA_HEADS, lw), 1)
    s = jnp.where(k_pos > past_len - WINDOW, _dot(q, kw_t), NEG)
    owin_ref[0] = _attend_with_new(q, s, [vw_t], new[:, 256:320], new[:, 320:384])
    eye = (lax.broadcasted_iota(jnp.int32, (NSA_HD, NSA_HD), 0)
           == lax.broadcasted_iota(jnp.int32, (NSA_HD, NSA_HD), 1))
    to_col = lambda r: jnp.sum(jnp.where(eye, r, 0.0), axis=1, keepdims=True)
    last = lax.broadcasted_iota(jnp.int32, (NSA_HD, lw), 1) == lw - 1
    newwin_ref[0, 0, 0] = jnp.where(last, to_col(new[:, 256:320]), pltpu.roll(kw_t, lw - 1, 1))
    newwin_ref[0, 0, 1] = jnp.where(last, to_col(new[:, 320:384]), pltpu.roll(vw_t, lw - 1, 1))


def _nsa_slc_sample(cache_kv_t, cache_win_t, layer, page_table, sel, qr, kvr):
    s, n_pages = page_table.shape
    past_len = n_pages * PAGE_SIZE
    lw = cache_win_t.shape[4]
    n_sel = SLC_TOPN - 1
    per_page = PAGE_SIZE // SLC_BLOCK

    def blk_spec(k):
        return pl.BlockSpec((1, 1, 2, NSA_HD, PAGE_SIZE),
                            lambda b, sl, pt: (layer, pt[b, sl[b, k] // per_page], 1, 0, 0))

    grid_spec = pltpu.PrefetchScalarGridSpec(
        num_scalar_prefetch=2, grid=(s,),
        in_specs=[blk_spec(k) for k in range(n_sel)]
        + [pl.BlockSpec((1, NSA_HEADS, NSA_HD), lambda b, sl, pt: (b, 0, 0)),
           pl.BlockSpec((1, 1, 384), lambda b, sl, pt: (b, 0, 0)),
           pl.BlockSpec((1, 1, 2, NSA_HD, lw), lambda b, sl, pt: (layer, b, 0, 0, 0))],
        out_specs=[pl.BlockSpec((1, NSA_HEADS, NSA_HD), lambda b, sl, pt: (b, 0, 0)),
                   pl.BlockSpec((1, NSA_HEADS, NSA_HD), lambda b, sl, pt: (b, 0, 0)),
                   pl.BlockSpec((1, 1, 2, NSA_HD, lw), lambda b, sl, pt: (0, b, 0, 0, 0))])
    return pl.pallas_call(
        functools.partial(_nsa_slc_sample_kernel, n_sel=n_sel, past_len=past_len),
        grid_spec=grid_spec,
        out_shape=[jax.ShapeDtypeStruct((s, NSA_HEADS, NSA_HD), F32),
                   jax.ShapeDtypeStruct((s, NSA_HEADS, NSA_HD), F32),
                   jax.ShapeDtypeStruct((1, s, 2, NSA_HD, lw), F32)],
        compiler_params=_cparams("parallel"), name="nsa_slc_sample",
    )(sel, page_table, *([cache_kv_t] * n_sel), qr, kvr, cache_win_t)


def _nsa_post_sample_kernel(ocmp_ref, oslc_ref, owin_ref, gate_ref, gx_ref, z_ref, out_ref):
    out_ref[...] = _nsa_combine(ocmp_ref[...], oslc_ref[...], owin_ref[...], gate_ref[...], gx_ref, z_ref[...])


def _nsa_post_sample(o_cmp, o_slc, o_win, gate, z):
    s = gate.shape[0]
    return pl.pallas_call(
        _nsa_post_sample_kernel, out_shape=jax.ShapeDtypeStruct((s, 512), BF16), name="nsa_post_sample",
    )(o_cmp, o_slc, o_win, gate, _gate_mats(), z)


def _lru_gates(u, wa_ref, ba_ref, wx_ref, bx_ref, lam_ref):
    r =jax.nn.sigmoid(_dot(u, wa_ref[...]) + ba_ref[...])
    ig = jax.nn.sigmoid(_dot(u, wx_ref[...]) + bx_ref[...])
    log_a = -LRU_C * r * jax.nn.softplus(-lam_ref[...])
    a = jnp.exp(log_a)
    b = jnp.sqrt(jnp.tanh(-log_a) * (a * a + 1.0)) * (ig * u)
    return a, b


def _lru_kernel(h_ref, cw_ref, cb_ref, wa_ref, ba_ref, wx_ref, bx_ref, lam_ref,
                br_ref, hlast_ref, conv_ref, xbuf, hc):
    t = pl.program_id(1)
    tt = h_ref.shape[1]

    @pl.when(t == 0)
    def _():
        xbuf[0:8, :] = jnp.zeros((8, BRANCH_W), F32)
        hc[...] = jnp.zeros_like(hc)

    x = h_ref[0, :, 0:512]
    z = h_ref[0, :, 512:1024]
    xbuf[8:8 + tt, :] = x
    u = cb_ref[...] + xbuf[pl.ds(8, tt), :] * cw_ref[3:4, :]
    for j in range(CONV_W - 1):
        u = u + xbuf[pl.ds(5 + j, tt), :] * cw_ref[j:j + 1, :]
    xbuf[0:8, :] = xbuf[tt:tt + 8, :]
    a, b = _lru_gates(u, wa_ref, ba_ref, wx_ref, bx_ref, lam_ref)
    row = lax.broadcasted_iota(jnp.int32, (tt, BRANCH_W), 0)
    d = 1
    while d < tt:
        keep = row >= d
        b = jnp.where(keep, a * pltpu.roll(b, d, 0) + b, b)
        a = jnp.where(keep, a * pltpu.roll(a, d, 0), a)
        d *= 2
    hseq = a * hc[...] + b
    hc[...] = hseq[tt - 1:tt, :]
    br_ref[0] = (hseq * _silu(z)).astype(BF16)

    @pl.when(t == pl.num_programs(1) - 1)
    def _():
        hlast_ref[0] = hseq[tt - 1:tt, :]
        conv_ref[0] = x[tt - (CONV_W - 1):tt, :]


def _block_diag(w):
    n, c, d = w.shape
    eye = jnp.eye(n, dtype=w.dtype)
    return (w[:, :, None, :] * eye[:, None, :, None]).reshape(n * c, n * d)


def _lru_weights(p):
    row = lambda v: v.reshape(1, BRANCH_W)
    return (p["lru_conv_w"], row(p["lru_conv_b"]), _block_diag(p["lru_wa"]).astype(BF16), row(p["lru_ba"]),
            _block_diag(p["lru_wx"]).astype(BF16), row(p["lru_bx"]), row(p["lru_lam"]))


def _lru_prompt(h, p, tt):
    b, t, _ = h.shape
    const = lambda r, w: pl.BlockSpec((r, w), lambda bi, i: (0, 0))
    return pl.pallas_call(
        _lru_kernel, grid=(b, t // tt),
        in_specs=[pl.BlockSpec((1, tt, SEG_LRU), lambda bi, i: (bi, i, 0)),
                  const(CONV_W, 512), const(1, 512), const(512, 512), const(1, 512), const(512, 512),
                  const(1, 512), const(1, 512)],
        out_specs=[pl.BlockSpec((1, tt, 512), lambda bi, i: (bi, i, 0)),
                   pl.BlockSpec((1, 1, 512), lambda bi, i: (bi, 0, 0)),
                   pl.BlockSpec((1, CONV_W - 1, 512), lambda bi, i: (bi, 0, 0))],
        out_shape=[jax.ShapeDtypeStruct((b, t, 512), BF16), jax.ShapeDtypeStruct((b, 1, 512), F32),
                   jax.ShapeDtypeStruct((b, CONV_W - 1, 512), F32)],
        scratch_shapes=[pltpu.VMEM((tt + 8, 512), F32), pltpu.VMEM((1, 512), F32)],
        compiler_params=_cparams("parallel", "arbitrary"), name="lru_prompt")(h, *_lru_weights(p))


def _lru_sample_kernel(h_ref, c0_ref, c1_ref, c2_ref, h0_ref, cw_ref, cb_ref, wa_ref, ba_ref, wx_ref, bx_ref,
                       lam_ref, br_ref, hnew_ref):
    x = h_ref[:, 0:512]
    z = h_ref[:, 512:1024]
    u = cb_ref[...] + x * cw_ref[3:4, :]
    for j, c_ref in enumerate((c0_ref, c1_ref, c2_ref)):
        u = u + c_ref[...] * cw_ref[j:j + 1, :]
    a, b = _lru_gates(u, wa_ref, ba_ref, wx_ref, bx_ref, lam_ref)
    hnew = a * h0_ref[...] + b
    hnew_ref[...] = hnew
    br_ref[...] = (hnew * _silu(z)).astype(BF16)


def _lru_sample(h, conv_prev_t, h0, p):
    s = h.shape[0]
    c0, c1, c2 = (conv_prev_t[j] for j in range(CONV_W - 1))
    return pl.pallas_call(
        _lru_sample_kernel,
        out_shape=[jax.ShapeDtypeStruct((s, 512), BF16), jax.ShapeDtypeStruct((s, 512), F32)],
        name="lru_sample")(h, c0, c1, c2, h0, *_lru_weights(p))


def _gla_constants(c):
    tri = np.tril(np.ones((c, c), np.float32))
    rows = [tri]
    masks = []
    t = np.arange(c)
    b = c // 2
    while b >= 1:
        ref_row = (t // (2 * b)) * 2 * b + b - 1
        rows.append(tri[ref_row])
        same = (t[:, None] // (2 * b)) == (t[None, :] // (2 * b))
        masks.append((same & ((t[:, None] // b) % 2 == 1) & ((t[None, :] // b) % 2 == 0)).astype(np.float32))
        b //= 2
    return (jnp.asarray(np.concatenate(rows, axis=0), dtype=BF16), jnp.asarray(np.stack(masks), dtype=F32))


def _gla_log_alpha(a_in, wa_ref, ba_ref):
    return jax.nn.log_sigmoid(_dot(a_in, wa_ref[...]) + ba_ref[...]) * (1.0 / GLA_TAU)


def _gla_out(o, gout_ref, z):
    outs = []
    for h in range(GLA_HEADS):
        oh = o[h]
        ms = jnp.mean(oh * oh, axis=-1, keepdims=True)
        outs.append(oh * lax.rsqrt(ms + NORM_EPS) * gout_ref[...])
    return (jnp.concatenate(outs, axis=1) * _silu(z)).astype(BF16)


def _gla_kernel(h_ref, wa_ref, ba_ref, gout_ref, cum_ref, lm_ref, br_ref, state_ref, st_s):
    ci = pl.program_id(1)
    c = h_ref.shape[1]
    n_lvl = lm_ref.shape[0]

    @pl.when(ci == 0)
    def _():
        st_s[...] = jnp.zeros_like(st_s)

    h = h_ref[0]
    q = h[:, 0:256] * (GLA_DK ** -0.5)
    k = h[:, 256:512]
    v = h[:, 512:1024]
    g = _gla_log_alpha(h[:, 1024:1152], wa_ref, ba_ref)
    cums = _x3_dot(cum_ref[...], g)
    bc = cums[0:c]
    b_end = bc[c - 1:c, :]
    q_in = q * jnp.exp(bc)
    k_end = k * jnp.exp(b_end - bc)
    aq = []
    ak = []
    for lv in range(n_lvl):
        ref = cums[(lv + 1) * c:(lv + 2) * c]
        aq.append(q * jnp.exp(jnp.minimum(bc - ref, 0.0)))
        ak.append(k * jnp.exp(jnp.minimum(ref - bc, 0.0)))
    eye = (lax.broadcasted_iota(jnp.int32, (c, c), 0) == lax.broadcasted_iota(jnp.int32, (c, c), 1))
    decay_end = jnp.exp(b_end)
    outs = []
    for hd in range(GLA_HEADS):
        ks = slice(GLA_DK * hd, GLA_DK * (hd + 1))
        vh = v[:, GLA_DV * hd:GLA_DV * (hd + 1)]
        att = jnp.where(eye, _dot_nt(q[:, ks], k[:, ks]), 0.0)
        for lv in range(n_lvl):
            att = att + lm_ref[lv] * _dot_nt(aq[lv][:, ks], ak[lv][:, ks])
        st = st_s[hd]
        outs.append(_dot_nt(q_in[:, ks], st) + _dot(att, vh))
        st_s[hd] = st * decay_end[:, ks] + _dot_tn(vh, k_end[:, ks])
    br_ref[0] = _gla_out(outs, gout_ref, h[:, 1152:1664])

    @pl.when(ci == pl.num_programs(1) - 1)
    def _():
        for hd in range(GLA_HEADS):
            state_ref[0, hd] = st_s[hd].T


def _gla_weights(p):
    wa = jnp.zeros((128, GLA_HEADS * GLA_DK), F32).at[:GLA_RANK].set(p["gla_w_a2"]).astype(BF16)
    return wa, p["gla_b_a"].reshape(1, -1), p["gla_out_g"].reshape(1, -1)


def _gla_prompt(h, p):
    b, t, _ = h.shape
    c = GLA_CHUNK
    cum, lm = _gla_constants(c)
    const2 = lambda r, w: pl.BlockSpec((r, w), lambda bi, i: (0, 0))
    return pl.pallas_call(
        _gla_kernel, grid=(b, t // c),
        in_specs=[pl.BlockSpec((1, c, SEG_GLA), lambda bi, i: (bi, i, 0)),
                  const2(128, 256), const2(1, 256), const2(1, GLA_DV), const2(cum.shape[0], c),
                  pl.BlockSpec(lm.shape, lambda bi, i: (0, 0, 0))],
        out_specs=[pl.BlockSpec((1, c, 512), lambda bi, i: (bi, i, 0)),
                   pl.BlockSpec((1, GLA_HEADS, GLA_DK, GLA_DV), lambda bi, i: (bi, 0, 0, 0))],
        out_shape=[jax.ShapeDtypeStruct((b, t, 512), BF16),
                   jax.ShapeDtypeStruct((b, GLA_HEADS, GLA_DK, GLA_DV), F32)],
        scratch_shapes=[pltpu.VMEM((GLA_HEADS, GLA_DV, GLA_DK), F32)],
        compiler_params=_cparams("parallel", "arbitrary"), name="gla_prompt")(h, *_gla_weights(p), cum, lm)


def _gla_sample_kernel(h_ref, s0_ref, wa_ref, ba_ref, gout_ref, br_ref, snew_ref):
    h = h_ref[0]
    q = h[:, 0:256] * (GLA_DK ** -0.5)
    k = h[:, 256:512]
    v = h[:, 512:1024]
    g = _gla_log_alpha(h[:, 1024:1152], wa_ref, ba_ref)
    decay = jnp.exp(g)
    eye = (lax.broadcasted_iota(jnp.int32, (GLA_DK, GLA_DK), 0)
           == lax.broadcasted_iota(jnp.int32, (GLA_DK, GLA_DK), 1))
    to_col = lambda r: jnp.sum(jnp.where(eye, r, 0.0), axis=1, keepdims=True)
    outs = []
    for hd in range(GLA_HEADS):
        ks = slice(GLA_DK * hd, GLA_DK * (hd + 1))
        vh = v[:, GLA_DV * hd:GLA_DV * (hd + 1)]
        s0 = s0_ref[0, 0, hd]
        qk = jnp.sum(q[:, ks] * k[:, ks], axis=-1, keepdims=True)
        outs.append(_dot(q[:, ks] * decay[:, ks], s0) + qk * vh)
        snew_ref[0, hd] = to_col(decay[:, ks]) * s0 + to_col(k[:, ks]) * vh
    br_ref[0] = _gla_out(outs, gout_ref, h[:, 1152:1664])


def _gla_sample(h, state, layer, p):
    s = h.shape[0]
    const2 = lambda r, w: pl.BlockSpec((r, w), lambda b: (0, 0))
    br, snew = pl.pallas_call(
        _gla_sample_kernel, grid=(s,),
        in_specs=[pl.BlockSpec((1, 1, SEG_GLA), lambda b: (b, 0, 0)),
                  pl.BlockSpec((1, 1, GLA_HEADS, GLA_DK, GLA_DV), lambda b: (layer, b, 0, 0, 0)),
                  const2(128, 256), const2(1, 256), const2(1, GLA_DV)],
        out_specs=[pl.BlockSpec((1, 1, 512), lambda b: (b, 0, 0)),
                   pl.BlockSpec((1, GLA_HEADS, GLA_DK, GLA_DV), lambda b: (b, 0, 0, 0))],
        out_shape=[jax.ShapeDtypeStruct((s, 1, 512), BF16),
                   jax.ShapeDtypeStruct((s, GLA_HEADS, GLA_DK, GLA_DV), F32)],
        compiler_params=_cparams("parallel"), name="gla_sample",
    )(h.reshape(s, 1, SEG_GLA), state, *_gla_weights(p))
    return br.reshape(s, 512), snew


def _mla_prep_kernel(h_ref, cq_g_ref, wuq_ref, grp_ref, qg_ref, wuk_ref, ckv_g_ref, kr_g_ref,
                     cosq_ref, sinq_ref, cosk_ref, sink_ref, qcat_ref, lat_ref):
    h = h_ref[0]
    cq = h[:, 0:384]
    cqn = cq * lax.rsqrt(jnp.mean(cq * cq, axis=-1, keepdims=True) + NORM_EPS) * cq_g_ref[...]
    q = _dot(cqn, wuq_ref[...])
    ms = _dot_x3(q * q, grp_ref[...]) * (1.0 / (MLA_NOPE + MLA_ROPE))
    qn = q * lax.rsqrt(ms + NORM_EPS) * qg_ref[...]
    q_rope = _rope64(qn[:, 512:768], cosq_ref[...], sinq_ref[...])
    for hd in range(MLA_HEADS):
        q_lat = _dot(qn[:, MLA_NOPE * hd:MLA_NOPE * (hd + 1)], wuk_ref[hd])
        qcat_ref[0, hd] = jnp.concatenate(
            [q_lat.T, q_rope[:, MLA_ROPE * hd:MLA_ROPE * (hd + 1)].T], axis=0).astype(BF16)
    ckv = h[:, 384:512]
    c = ckv * lax.rsqrt(jnp.mean(ckv * ckv, axis=-1, keepdims=True) + NORM_EPS) * ckv_g_ref[...]
    kr = h[:, 512:640]
    ms_kr = jnp.sum(kr * kr, axis=-1, keepdims=True) * (1.0 / MLA_ROPE)
    krn = kr * lax.rsqrt(ms_kr + NORM_EPS) * kr_g_ref[...]
    krr = _rope64(krn, cosk_ref[...], sink_ref[...])
    lat_ref[0] = jnp.concatenate([c, krr[:, 0:MLA_ROPE]], axis=1)


def _mla_weights(p):
    hw = MLA_NOPE + MLA_ROPE
    w = p["mla_w_uq"].reshape(MLA_Q_LORA, MLA_HEADS, hw)
    wuq = jnp.concatenate([w[:, :, :MLA_NOPE].reshape(MLA_Q_LORA, -1),
                           w[:, :, MLA_NOPE:].reshape(MLA_Q_LORA, -1)], axis=1).astype(BF16)
    qg = jnp.concatenate([jnp.tile(p["mla_q_g"][:MLA_NOPE], MLA_HEADS),
                          jnp.tile(p["mla_q_g"][MLA_NOPE:], MLA_HEADS)]).reshape(1, -1)
    head = np.concatenate([np.arange(MLA_HEADS * MLA_NOPE) // MLA_NOPE, np.arange(MLA_HEADS * MLA_ROPE) // MLA_ROPE])
    grp = jnp.asarray((head[:, None] == head[None, :]).astype(np.float32), dtype=BF16)
    wuk_t = jnp.swapaxes(p["mla_w_uk"], 1, 2).astype(BF16)
    kr_g = jnp.concatenate([p["mla_kr_g"], jnp.zeros((128 - MLA_ROPE,), F32)]).reshape(1, 128)
    return (p["mla_cq_g"].reshape(1, -1), wuq, grp, qg, wuk_t, p["mla_ckv_g"].reshape(1, -1), kr_g)


def _mla_prep(h, tabs, p, tq):
    b, t, _ = h.shape
    tab = lambda w: pl.BlockSpec((tq, w), lambda bi, i: (i, 0))
    const2 = lambda r, w: pl.BlockSpec((r, w), lambda bi, i: (0, 0))
    return pl.pallas_call(
        _mla_prep_kernel, grid=(b, t // tq),
        in_specs=[pl.BlockSpec((1, tq, SEG_MLA), lambda bi, i: (bi, i, 0)),
                  const2(1, 384), const2(384, 768), const2(768, 768), const2(1, 768),
                  pl.BlockSpec((MLA_HEADS, MLA_NOPE, MLA_KV_LORA), lambda bi, i: (0, 0, 0)),
                  const2(1, 128), const2(1, 128), tab(256), tab(256), tab(128), tab(128)],
        out_specs=[pl.BlockSpec((1, MLA_HEADS, MLA_ROW, tq), lambda bi, i: (bi, 0, 0, i)),
                   pl.BlockSpec((1, tq, MLA_ROW), lambda bi, i: (bi, i, 0))],
        out_shape=[jax.ShapeDtypeStruct((b, MLA_HEADS, MLA_ROW, t), BF16),
                   jax.ShapeDtypeStruct((b, t, MLA_ROW), F32)],
        compiler_params=_cparams("parallel", "parallel"), name="mla_prep")(h, *_mla_weights(p), *tabs)


def _mla_up(o_lat, wuv_ref, z):
    outs = [_dot(o_lat[hd], wuv_ref[hd]) for hd in range(MLA_HEADS)]
    return (jnp.concatenate(outs, axis=1) * _silu(z)).astype(BF16)


def _mla_attn_kernel(q_ref, lat_ref, wuv_ref, z_ref, out_ref, m_s, l_s, acc_s, *, tq):
    i = pl.program_id(1)
    q_t = jnp.concatenate([q_ref[0, hd] for hd in range(MLA_HEADS)], axis=1)
    stats = (m_s, l_s, acc_s)
    _flash_reset(*stats)

    def tile(j, bias):
        start = pl.multiple_of(j * tq, tq)
        lat = lat_ref[0, pl.ds(start, tq), :].astype(BF16)
        s_t = _dot(lat, q_t) * MLA_SCALE
        if bias is not None:
            s_t = s_t + jnp.concatenate([bias] * MLA_HEADS, axis=1)
        _flash_update(s_t, lat[:, 0:MLA_KV_LORA], *stats)

    def full_tile(j, carry):
        tile(j, None)
        return carry

    lax.fori_loop(0, i, full_tile, 0)
    causal = (lax.broadcasted_iota(jnp.int32, (tq, tq), 0) <= lax.broadcasted_iota(jnp.int32, (tq, tq), 1))
    tile(i, jnp.where(causal, 0.0, NEG))
    o_t = acc_s[...] / l_s[...]
    out_ref[0] = _mla_up([o_t[:, tq * hd:tq * (hd + 1)].T for hd in range(MLA_HEADS)], wuv_ref, z_ref[0])


def _mla_attn(qcat_t, lat, w_uv, h, tq):
    b, _, _, t = qcat_t.shape
    cols = MLA_HEADS * tq
    return pl.pallas_call(
        functools.partial(_mla_attn_kernel, tq=tq), grid=(b, t // tq),
        in_specs=[pl.BlockSpec((1, MLA_HEADS, MLA_ROW, tq), lambda bi, i: (bi, 0, 0, i)),
                  pl.BlockSpec((1, t, MLA_ROW), lambda bi, i: (bi, 0, 0)),
                  pl.BlockSpec((MLA_HEADS, MLA_KV_LORA, MLA_VHD), lambda bi, i: (0, 0, 0)),
                  pl.BlockSpec((1, tq, 512), lambda bi, i: (bi, i, 0))],
        out_specs=pl.BlockSpec((1, tq, 512), lambda bi, i: (bi, i, 0)),
        out_shape=jax.ShapeDtypeStruct((b, t, 512), BF16),
        scratch_shapes=[pltpu.VMEM((1, cols), F32), pltpu.VMEM((1, cols), F32),
                        pltpu.VMEM((MLA_KV_LORA, cols), F32)],
        compiler_params=_cparams("parallel", "arbitrary"), name="mla_attn")(qcat_t, lat, w_uv.astype(BF16), h)


def _mla_sample_kernel(pt_ref, cache_ref, q_ref, new_ref, o_ref, buf, sem, m_s, l_s, acc_s, *, layer, n_pages):
    group = buf.shape[1]
    n_groups = n_pages // group

    def compute(b, g, slot):
        q = q_ref[b]
        first = g == 0
        m_run = jnp.where(first, NEG, m_s[...])
        l_run = jnp.where(first, 0.0, l_s[...])
        acc_run = jnp.where(first, 0.0, acc_s[...])
        pages = [buf[slot, k].astype(BF16) for k in range(group)]
        scores = [_dot(q, pg) * MLA_SCALE for pg in pages]
        m_pg = [jnp.max(s, axis=-1, keepdims=True) for s in scores]
        probs = [jnp.exp(s - mk) for s, mk in zip(scores, m_pg)]
        l_pg = [jnp.sum(p, axis=-1, keepdims=True) for p in probs]
        o_pg = [_dot_nt(p, pg[0:MLA_KV_LORA, :]) for p, pg in zip(probs, pages)]
        m_new = functools.reduce(jnp.maximum, m_pg, m_run)
        w_run = jnp.exp(m_run - m_new)
        l_new = w_run * l_run
        acc = w_run * acc_run
        for mk, lk, ok in zip(m_pg, l_pg, o_pg):
            w = jnp.exp(mk - m_new)
            l_new = l_new + w * lk
            acc = acc + w * ok
        m_s[...] = m_new
        l_s[...] = l_new
        acc_s[...] = acc

        @pl.when(g == n_groups - 1)
        def _():
            new = new_ref[b]
            s_new = jnp.sum(q.astype(F32) * new, axis=-1, keepdims=True) * MLA_SCALE
            m_fin = jnp.maximum(m_new, s_new)
            a_fin = jnp.exp(m_new - m_fin)
            p_new = jnp.exp(s_new - m_fin)
            o_ref[b] = (a_fin * acc + p_new * new[:, 0:MLA_KV_LORA]) / (a_fin * l_new + p_new)

    m_s[...] = jnp.full_like(m_s, NEG)
    l_s[...] = jnp.zeros_like(l_s)
    acc_s[...] = jnp.zeros_like(acc_s)
    _page_stream(pt_ref, buf, sem, n_groups, group, lambda page: cache_ref.at[layer, page], compute)


def _mla_sample_attn(cache_mla_t, layer, page_table, q, new_rows, pages_per_step):
    s, n_pages = page_table.shape
    group = pages_per_step
    assert n_pages % group == 0
    vmem = pl.BlockSpec(memory_space=pltpu.VMEM)
    return pl.pallas_call(
        functools.partial(_mla_sample_kernel, layer=layer, n_pages=n_pages),
        in_specs=[pl.BlockSpec(memory_space=pltpu.SMEM), pl.BlockSpec(memory_space=pl.ANY), vmem, vmem],
        out_specs=vmem,
        out_shape=jax.ShapeDtypeStruct((s, MLA_HEADS, MLA_KV_LORA), F32),
        scratch_shapes=[pltpu.VMEM((STREAM_SLOTS, group, MLA_ROW, PAGE_SIZE), F32),
                        pltpu.SemaphoreType.DMA((STREAM_SLOTS,)),
                        pltpu.VMEM((MLA_HEADS, 1), F32), pltpu.VMEM((MLA_HEADS, 1), F32),
                        pltpu.VMEM((MLA_HEADS, MLA_KV_LORA), F32)],
        compiler_params=pltpu.CompilerParams(vmem_limit_bytes=VMEM_LIMIT), name="mla_sample_attn",
    )(page_table, cache_mla_t, q, new_rows)


def _mla_post_sample_kernel(o_ref, wuv_ref, z_ref, out_ref):
    out_ref[...] = _mla_up([o_ref[:, hd, :] for hd in range(MLA_HEADS)], wuv_ref, z_ref[...])


def _mla_post_sample(o_lat, w_uv, z):
    s = o_lat.shape[0]
    return pl.pallas_call(
        _mla_post_sample_kernel, out_shape=jax.ShapeDtypeStruct((s, 512), BF16), name="mla_post_sample",
    )(o_lat, w_uv.astype(BF16), z)


def _pad_cols(w, width):
    return jnp.pad(w, ((0, 0), (0, width - w.shape[1])))


def _input_weights(w_in):
    sizes = (512, 384, 24, 512, 512, 512, 256, 256, 512, 16, 512, 384, 128, 64, 512)
    offs = np.concatenate([[0], np.cumsum(sizes)])
    seg = [w_in[:, offs[i]:offs[i + 1]] for i in range(len(sizes))]
    (nsa_q, nsa_kv, nsa_g, nsa_z, lru_x, lru_z, gla_q, gla_k, gla_v, gla_a, gla_z,
     mla_cq, mla_ckv, mla_kr, mla_z) = seg
    w_nsa = jnp.concatenate([nsa_q, nsa_kv, _pad_cols(nsa_g, 128), nsa_z], axis=1)
    w_lru = jnp.concatenate([lru_x, lru_z], axis=1)
    w_gla = jnp.concatenate([gla_q, gla_k, gla_v, _pad_cols(gla_a, 128), gla_z], axis=1)
    w_mla = jnp.concatenate([mla_cq, mla_ckv, _pad_cols(mla_kr, 128), mla_z], axis=1)
    w_merge = w_in[:, offs[-1]:]
    return tuple(w.astype(BF16) for w in (w_nsa, w_lru, w_gla, w_mla, w_merge))


def _layer_params(params, layer):
    return {k: v[layer] for k, v in params.items()}


def _prompt_layer(x, p, tabs):
    b, t, d = x.shape
    n = b * t
    w_nsa, w_lru, w_gla, w_mla, w_merge = _input_weights(p["w_in"])
    xn = _rmsnorm_bf16(x.reshape(n, d), p["norm_g"], 512)
    seg = lambda w, name: _matmul(xn, w, 512, name).reshape(b, t, -1)
    h_nsa, h_lru, h_gla, h_mla = (seg(w_nsa, "in_nsa"), seg(w_lru, "in_lru"), seg(w_gla, "in_gla"),
                                  seg(w_mla, "in_mla"))
    nsa_tabs, mla_tabs = tabs
    tq = 128
    qn, qr, rows, win, gate = _nsa_prep(h_nsa, nsa_tabs, p["nsa_q_g"], p["nsa_k_g"], tq)
    comp = _compress(rows, p["nsa_cmp_w"])
    o_cmp, sel = _nsa_cmp(qn, comp, tq)
    br_nsa = _nsa_attn(qr, sel, rows, win, o_cmp, gate, h_nsa, tq)
    br_lru, h_last, conv_new = _lru_prompt(h_lru, p, 256)
    br_gla, gla_state = _gla_prompt(h_gla, p)
    qcat, lat = _mla_prep(h_mla, mla_tabs, p, tq)
    br_mla = _mla_attn(qcat, lat, p["mla_w_uv"], h_mla[..., 640:1152], 256)
    flat = lambda a: a.reshape(n, BRANCH_W)
    merged = _merge(xn, [flat(br_nsa), flat(br_lru), flat(br_gla), flat(br_mla)], w_merge,
                    p["w_branch"].astype(BF16), 256)
    y = _outproj(merged, p["w_out"].astype(BF16), x.reshape(n, d), 256).reshape(b, t, d)
    keep = min(WINDOW, t)
    states = (rows.reshape(b, t, 4, NSA_HD), win[:, t - keep:].reshape(b, keep, 2, NSA_HD),
              h_last.reshape(b, BRANCH_W), conv_new, gla_state, lat)
    return y, states


def _sample_layer(x, p, layer, caches, tabs):
    s, t, d = x.shape
    cache_nsa_kv, cache_nsa_win, state_lru_h, state_lru_conv, state_gla, cache_mla, page_table = caches
    w_nsa, w_lru, w_gla, w_mla, w_merge = _input_weights(p["w_in"])
    x2 = x.reshape(s, d)
    xn = _rmsnorm_bf16(x2, p["norm_g"], s)
    h_nsa = _matmul(xn, w_nsa, s, "in_nsa_s")
    h_lru = _matmul(xn, w_lru, s, "in_lru_s")
    h_gla = _matmul(xn, w_gla, s, "in_gla_s")
    h_mla = _matmul(xn, w_mla, s, "in_mla_s")
    nsa_tabs, mla_tabs = tabs
    qn, qr, rows, win, gate = _nsa_prep(h_nsa[None], nsa_tabs, p["nsa_q_g"], p["nsa_k_g"], s)
    heads = lambda a: a.reshape(s, NSA_HEADS, NSA_HD)
    pages_per_step = min(STREAM_GROUP_PAGES, page_table.shape[1])
    o_cmp, sel = _nsa_cmp_sample(cache_nsa_kv, layer, page_table, heads(qn[0]), p["nsa_cmp_w"], pages_per_step)
    kvr = jnp.concatenate([rows[0], win[0]], axis=1).reshape(s, 1, 384)
    o_slc, o_win, new_win_t = _nsa_slc_sample(cache_nsa_kv, cache_nsa_win, layer, page_table,
                                              sel.reshape(s, 128), heads(qr[0]), kvr)
    flat = lambda a: a.reshape(s, BRANCH_W)
    br_nsa = _nsa_post_sample(flat(o_cmp), flat(o_slc), flat(o_win), gate[0], h_nsa[:, 1024:1536])
    br_lru, h_new = _lru_sample(h_lru, state_lru_conv[layer], state_lru_h[layer], p)
    conv_new = jnp.swapaxes(jnp.stack([state_lru_conv[layer, 1], state_lru_conv[layer, 2], h_lru[:, 0:512]]), 0, 1)
    br_gla, gla_state = _gla_sample(h_gla, state_gla, layer, p)
    qcat_t, lat = _mla_prep(h_mla[None], mla_tabs, p, s)
    o_lat = _mla_sample_attn(cache_mla, layer, page_table, jnp.transpose(qcat_t[0], (2, 0, 1)),
                             lat[0].reshape(s, 1, MLA_ROW), pages_per_step)
    br_mla = _mla_post_sample(o_lat, p["mla_w_uv"], h_mla[:, 640:1152])
    merged = _merge(xn, [br_nsa, br_lru, br_gla, br_mla], w_merge, p["w_branch"].astype(BF16), s)
    y = _outproj(merged, p["w_out"].astype(BF16), x2, s).reshape(s, t, d)
    states = (rows[0].reshape(s, 1, 4, NSA_HD), jnp.transpose(new_win_t[0], (0, 3, 1, 2)), h_new, conv_new,
              gla_state, lat[0].reshape(s, 1, MLA_ROW))
    return y, states


def kernel(x_prompt, x_sample, cache_nsa_kv, cache_nsa_win, state_lru_h, state_lru_conv, state_gla, cache_mla,
           page_table, norm_g, w_in, nsa_q_g, nsa_k_g, nsa_cmp_w, lru_conv_w, lru_conv_b, lru_wa, lru_ba, lru_wx,
           lru_bx, lru_lam, gla_w_a2, gla_b_a, gla_out_g, mla_cq_g, mla_w_uq, mla_q_g, mla_ckv_g, mla_kr_g,
           mla_w_uk, mla_w_uv, w_branch, w_out):
    params = dict(norm_g=norm_g, w_in=w_in, nsa_q_g=nsa_q_g, nsa_k_g=nsa_k_g, nsa_cmp_w=nsa_cmp_w,
                  lru_conv_w=lru_conv_w, lru_conv_b=lru_conv_b, lru_wa=lru_wa, lru_ba=lru_ba, lru_wx=lru_wx,
                  lru_bx=lru_bx, lru_lam=lru_lam, gla_w_a2=gla_w_a2, gla_b_a=gla_b_a, gla_out_g=gla_out_g,
                  mla_cq_g=mla_cq_g, mla_w_uq=mla_w_uq, mla_q_g=mla_q_g, mla_ckv_g=mla_ckv_g,
                  mla_kr_g=mla_kr_g, mla_w_uk=mla_w_uk, mla_w_uv=mla_w_uv, w_branch=w_branch, w_out=w_out)
    depth = w_in.shape[0]
    assert x_prompt.shape[2] == N_BRANCH * BRANCH_W and x_sample.shape[1] == 1
    t = x_prompt.shape[1]
    n_seq, n_pages = page_table.shape
    past_len = n_pages * PAGE_SIZE
    assert t % 256 == 0 and past_len % SLC_BLOCK == 0 and cache_nsa_win.shape[2] == WINDOW
    caches = (jnp.transpose(cache_nsa_kv, (0, 1, 3, 4, 2)), jnp.transpose(cache_nsa_win, (0, 1, 3, 4, 2)),
              state_lru_h, jnp.transpose(state_lru_conv, (0, 2, 1, 3)), state_gla,
              jnp.transpose(cache_mla, (0, 1, 3, 2)), page_table)
    tabs_p = _rope_tables(jnp.arange(t))
    tabs_s = _rope_tables(jnp.full((n_seq,), past_len))
    y_p, y_s = x_prompt, x_sample
    sp, ss = [], []
    for layer in range(depth):
        p = _layer_params(params, layer)
        y_p, st = _prompt_layer(y_p, p, tabs_p)
        sp.append(st)
        y_s, st = _sample_layer(y_s, p, layer, caches, tabs_s)
        ss.append(st)
    stack = lambda lst, i: jnp.stack([s[i] for s in lst])
    return (y_p, y_s, stack(sp, 0), stack(ss, 0), stack(sp, 1), stack(ss, 1), stack(sp, 2), stack(ss, 2),
            stack(sp, 3), stack(ss, 3), stack(sp, 4), stack(ss, 4), stack(sp, 5), stack(ss, 5))
```

```python
import functools

import numpy as np
import jax
import jax.numpy as jnp
from jax import lax
from jax.experimental import pallas as pl
from jax.experimental.pallas import tpu as pltpu

F32 = jnp.float32
BF16 = jnp.bfloat16
NEG = -1e30

N_BRANCH = 4
NSA_HEADS = 8
NSA_HD = 64
CMP_BLOCK = 32
CMP_STRIDE = 16
SLC_BLOCK = 64
SLC_TOPN = 16
WINDOW = 512
PAGE_SIZE = 128
LRU_BLOCKS = 8
CONV_W = 4
LRU_C = 8.0
GLA_HEADS = 4
GLA_DK = 64
GLA_DV = 128
GLA_RANK = 16
GLA_TAU = 16.0
GLA_CHUNK = 128
MLA_HEADS = 4
MLA_NOPE = 128
MLA_ROPE = 64
MLA_VHD = 128
MLA_Q_LORA = 384
MLA_KV_LORA = 128
MLA_ROW = MLA_KV_LORA + MLA_ROPE
ROPE_THETA = 10000.0
NORM_EPS = 1e-6

BRANCH_W = 512
NSA_SCALE = NSA_HD ** -0.5
MLA_SCALE = (MLA_NOPE + MLA_ROPE) ** -0.5

SEG_NSA = 1536
SEG_LRU = 1024
SEG_GLA = 1664
SEG_MLA = 1152

VMEM_LIMIT = 48 * 1024 * 1024
CMP_STREAM_PAGES = 16
MLA_STREAM_PAGES = 32
CMP_STREAM_SLOTS = 6
MLA_STREAM_SLOTS = 4
GLA_SAMPLE_SEQS = 4
MERGE_ROWS = 1024
MERGE_COLS = 1024


def _cparams(*sem):
    return pltpu.CompilerParams(dimension_semantics=sem, vmem_limit_bytes=VMEM_LIMIT)


def _dot(a, b):
    return jnp.dot(a.astype(BF16), b.astype(BF16), preferred_element_type=F32)


def _dot_nt(a, b):
    return lax.dot_general(a.astype(BF16), b.astype(BF16), (((1,), (1,)), ((), ())),
                           preferred_element_type=F32)


def _dot_tn(a, b):
    return lax.dot_general(a.astype(BF16), b.astype(BF16), (((0,), (0,)), ((), ())),
                           preferred_element_type=F32)


def _split3(a):
    a0 = a.astype(BF16)
    r = a - a0.astype(F32)
    a1 = r.astype(BF16)
    a2 = (r - a1.astype(F32)).astype(BF16)
    return a0, a1, a2


def _dot_x3(a, b_exact):
    a0, a1, a2 = _split3(a)
    d = functools.partial(jnp.dot, preferred_element_type=F32)
    return d(a0, b_exact) + d(a1, b_exact) + d(a2, b_exact)


def _x3_dot(a_exact, b):
    b0, b1, b2 = _split3(b)
    d = functools.partial(jnp.dot, preferred_element_type=F32)
    return d(a_exact, b0) + d(a_exact, b1) + d(a_exact, b2)


def _rope64(x, cos, sin_signed):
    w = x.shape[1]
    lane = lax.broadcasted_iota(jnp.int32, x.shape, 1)
    swapped = jnp.where((lane & 63) < 32, pltpu.roll(x, w - 32, 1), pltpu.roll(x, 32, 1))
    return x * cos + swapped * sin_signed


def _heads_to_rows(x, n_heads, hd):
    return jnp.concatenate([x[:, hd * h:hd * (h + 1)] for h in range(n_heads)], axis=0)


def _rows_to_heads(x, n_heads):
    t = x.shape[0] // n_heads
    return jnp.concatenate([x[t * h:t * (h + 1)] for h in range(n_heads)], axis=1)


def _silu(z):
    return z * jax.nn.sigmoid(z)


def _norm_kernel(x_ref, g_ref, o_ref):
    x = x_ref[...]
    ms = jnp.mean(x * x, axis=-1, keepdims=True)
    o_ref[...] = (x * lax.rsqrt(ms + NORM_EPS) * g_ref[...]).astype(o_ref.dtype)


def _rmsnorm_bf16(x, g, tm):
    n, d = x.shape
    return pl.pallas_call(
        _norm_kernel, grid=(n // tm,),
        in_specs=[pl.BlockSpec((tm, d), lambda i: (i, 0)), pl.BlockSpec((1, d), lambda i: (0, 0))],
        out_specs=pl.BlockSpec((tm, d), lambda i: (i, 0)),
        out_shape=jax.ShapeDtypeStruct((n, d), BF16),
        compiler_params=_cparams("parallel"), name="rmsnorm")(x, g.reshape(1, d))


def _mm_kernel(x_ref, w_ref, o_ref):
    o_ref[...] = jnp.dot(x_ref[...], w_ref[...], preferred_element_type=F32)


def _matmul(x, w, tm, name):
    n, k = x.shape
    c = w.shape[1]
    return pl.pallas_call(
        _mm_kernel, grid=(n // tm,),
        in_specs=[pl.BlockSpec((tm, k), lambda i: (i, 0)), pl.BlockSpec((k, c), lambda i: (0, 0))],
        out_specs=pl.BlockSpec((tm, c), lambda i: (i, 0)),
        out_shape=jax.ShapeDtypeStruct((n, c), F32),
        compiler_params=_cparams("parallel"), name=name)(x, w)


def _merge_kernel(xn_ref, b0_ref, b1_ref, b2_ref, b3_ref, wm_ref, wb_ref, o_ref, acc_ref):
    n = pl.program_id(2)

    @pl.when(n == 0)
    def _():
        acc_ref[...] = jnp.zeros_like(acc_ref)

    gate = jax.nn.sigmoid(jnp.dot(xn_ref[...], wm_ref[...], preferred_element_type=F32))
    for k, b_ref in enumerate((b0_ref, b1_ref, b2_ref, b3_ref)):
        @pl.when(n == k)
        def _(b_ref=b_ref):
            proj = jnp.dot(b_ref[...], wb_ref[0], preferred_element_type=F32)
            acc_ref[...] += gate * proj

    @pl.when(n == N_BRANCH - 1)
    def _():
        o_ref[...] = acc_ref[...].astype(o_ref.dtype)


def _merge(xn, branches, w_merge, w_branch, tm, tn):
    n, d = xn.shape
    bw = branches[0].shape[1]
    cols = d // tn
    bspec = pl.BlockSpec((tm, bw), lambda i, c, j: (i, 0))
    return pl.pallas_call(
        _merge_kernel, grid=(n // tm, cols, N_BRANCH),
        in_specs=[pl.BlockSpec((tm, d), lambda i, c, j: (i, 0)), bspec, bspec, bspec, bspec,
                  pl.BlockSpec((d, tn), lambda i, c, j: (0, j * cols + c)),
                  pl.BlockSpec((1, bw, tn), lambda i, c, j: (j, 0, c))],
        out_specs=pl.BlockSpec((tm, tn), lambda i, c, j: (i, c)),
        out_shape=jax.ShapeDtypeStruct((n, d), BF16),
        scratch_shapes=[pltpu.VMEM((tm, tn), F32)],
        compiler_params=_cparams("parallel", "parallel", "arbitrary"), name="merge",
    )(xn, *branches, w_merge, w_branch)


def _outproj_kernel(m_ref, w_ref, x_ref, o_ref):
    o_ref[...] = x_ref[...] + jnp.dot(m_ref[...], w_ref[...], preferred_element_type=F32)


def _outproj(merged, w_out, x, tm):
    n, d = x.shape
    return pl.pallas_call(
        _outproj_kernel, grid=(n // tm,),
        in_specs=[pl.BlockSpec((tm, d), lambda i: (i, 0)), pl.BlockSpec((d, d), lambda i: (0, 0)),
                  pl.BlockSpec((tm, d), lambda i: (i, 0))],
        out_specs=pl.BlockSpec((tm, d), lambda i: (i, 0)),
        out_shape=jax.ShapeDtypeStruct((n, d), F32),
        compiler_params=_cparams("parallel"), name="outproj")(merged, w_out, x)


def _nsa_prep_kernel(h_ref, cq_ref, sq_ref, ckv_ref, skv_ref, qg_ref, kvg_ref, kvm_ref, bdq_ref, bdkv_ref,
                     qn_ref, qr_ref, rows_ref, win_ref, gate_ref):
    h = h_ref[0]
    q = h[:, 0:512]
    kv = h[:, 512:896]
    ms_q = _dot_x3(q * q, bdq_ref[...]) * (1.0 / NSA_HD)
    qn = q * lax.rsqrt(ms_q + NORM_EPS) * qg_ref[...]
    qn_ref[0] = qn
    qr_ref[0] = _rope64(qn, cq_ref[...], sq_ref[...])
    ms_kv = _dot_x3(kv * kv, bdkv_ref[...]) * (1.0 / NSA_HD)
    kvn = kv * lax.rsqrt(ms_kv + NORM_EPS) * kvg_ref[...]
    kvn = jnp.where(kvm_ref[...] > 0.5, kvn, kv)
    kvr = _rope64(kvn, ckv_ref[...], skv_ref[...])
    rows_ref[0] = kvr[:, 0:256]
    win_ref[0] = kvr[:, 256:384]
    gate_ref[0] = jax.nn.sigmoid(h[:, 896:1024])


def _block_diag_ones(width, group):
    idx = np.arange(width) // group
    return jnp.asarray((idx[:, None] == idx[None, :]).astype(np.float32), dtype=BF16)


def _nsa_prep(h, tabs, q_gain, k_gain, tq):
    b, t, _ = h.shape
    cq, sq, ckv, skv = tabs
    one = jnp.ones((NSA_HD,), F32)
    kvg = jnp.concatenate([k_gain[0], one, k_gain[1], one, k_gain[2], one]).reshape(1, 384)
    kvm = jnp.concatenate([one, 0 * one, one, 0 * one, one, 0 * one]).reshape(1, 384)
    qg = jnp.tile(q_gain, NSA_HEADS).reshape(1, 512)
    tile = lambda w: pl.BlockSpec((1, tq, w), lambda bi, i: (bi, i, 0))
    tab = lambda w: pl.BlockSpec((tq, w), lambda bi, i: (i, 0))
    const = lambda r, w: pl.BlockSpec((r, w), lambda bi, i: (0, 0))
    return pl.pallas_call(
        _nsa_prep_kernel, grid=(b, t // tq),
        in_specs=[tile(SEG_NSA), tab(512), tab(512), tab(384), tab(384), const(1, 512), const(1, 384),
                  const(1, 384), const(512, 512), const(384, 384)],
        out_specs=[tile(512), tile(512), tile(256), tile(128), tile(128)],
        out_shape=[jax.ShapeDtypeStruct((b, t, 512), F32), jax.ShapeDtypeStruct((b, t, 512), F32),
                   jax.ShapeDtypeStruct((b, t, 256), F32), jax.ShapeDtypeStruct((b, t, 128), F32),
                   jax.ShapeDtypeStruct((b, t, 128), F32)],
        compiler_params=_cparams("parallel", "parallel"), name="nsa_prep",
    )(h, cq, sq, ckv, skv, qg, kvg, kvm, _block_diag_ones(512, 64), _block_diag_ones(384, 64))


def _rope_base(pos):
    inv = ROPE_THETA ** (-jnp.arange(0, 64, 2, dtype=F32) / 64)
    ang = pos.astype(F32)[:, None] * inv[None, :]
    c = jnp.cos(ang)
    s = jnp.sin(ang)
    return jnp.concatenate([c, c], axis=-1), jnp.concatenate([-s, s], axis=-1)


def _rope_tables(pos):
    c, s = _rope_base(pos)
    one = jnp.ones_like(c)
    zero = jnp.zeros_like(s)
    nsa = (jnp.tile(c, (1, NSA_HEADS)), jnp.tile(s, (1, NSA_HEADS)),
           jnp.concatenate([one, one, c, one, c, one], axis=1),
           jnp.concatenate([zero, zero, s, zero, s, zero], axis=1))
    mla = (jnp.tile(c, (1, MLA_HEADS)), jnp.tile(s, (1, MLA_HEADS)),
           jnp.concatenate([c, one], axis=1), jnp.concatenate([s, zero], axis=1))
    return nsa, mla


def _compress_kernel(rows_ref, w1_ref, w2_ref, comp_ref):
    t = rows_ref.shape[1]
    n = t // CMP_STRIDE
    first = jnp.zeros((n, 128), F32)
    second = jnp.zeros((n, 128), F32)
    for j in range(CMP_STRIDE):
        r = rows_ref[0, pl.ds(j, n, stride=CMP_STRIDE), :]
        first = first + r * w1_ref[j:j + 1, :]
        second = second + r * w2_ref[j:j + 1, :]
    nxt = pltpu.roll(second, n - 1, 0)
    row = lax.broadcasted_iota(jnp.int32, (n, 128), 0)
    comp_ref[0] = jnp.where(row < n - 1, first + nxt, 0.0)


def _cmp_weights(w_pos):
    w1 = jnp.concatenate([w_pos[0, :CMP_STRIDE], w_pos[1, :CMP_STRIDE]], axis=1)
    w2 = jnp.concatenate([w_pos[0, CMP_STRIDE:], w_pos[1, CMP_STRIDE:]], axis=1)
    return w1, w2


def _compress(rows, w_pos):
    b, t, _ = rows.shape
    w1, w2 = _cmp_weights(w_pos)
    n = t // CMP_STRIDE
    return pl.pallas_call(
        _compress_kernel, grid=(b,),
        in_specs=[pl.BlockSpec((1, t, 128), lambda bi: (bi, 0, 0)),
                  pl.BlockSpec((CMP_STRIDE, 128), lambda bi: (0, 0)),
                  pl.BlockSpec((CMP_STRIDE, 128), lambda bi: (0, 0))],
        out_specs=pl.BlockSpec((1, n, 128), lambda bi: (bi, 0, 0)),
        out_shape=jax.ShapeDtypeStruct((b, n, 128), F32),
        compiler_params=_cparams("parallel"), name="nsa_compress")(rows, w1, w2)


def _score_matrix(n_cmp, n_slc):
    m = np.zeros((n_cmp, n_slc), np.float32)
    per = SLC_BLOCK // CMP_STRIDE
    for n in range(n_cmp):
        for s in (n, n + 1):
            if s // per < n_slc:
                m[n, s // per] += 1.0
    return jnp.asarray(m, dtype=BF16)


def _nsa_cmp_kernel(qn_ref, comp_ref, m_ref, o_ref, sel_ref, *, tq):
    i = pl.program_id(1)
    qh = _heads_to_rows(qn_ref[0], NSA_HEADS, NSA_HD)
    comp = comp_ref[0]
    nc = comp.shape[0]
    s = _dot_nt(qh, comp[:, 0:64]) * NSA_SCALE
    qpos = i * tq + lax.broadcasted_iota(jnp.int32, (tq, nc), 0)
    blk_end = lax.broadcasted_iota(jnp.int32, (tq, nc), 1) * CMP_STRIDE + (CMP_BLOCK - 1)
    mask = (blk_end <= qpos)[None]
    s3 = jnp.where(mask, s.reshape(NSA_HEADS, tq, nc), NEG)
    mx = jnp.max(s3, axis=-1, keepdims=True)
    e = jnp.where(mask, jnp.exp(s3 - mx), 0.0)
    p3 = e / jnp.maximum(jnp.sum(e, axis=-1, keepdims=True), 1e-30)
    o = _dot(p3.reshape(NSA_HEADS * tq, nc), comp[:, 64:128])
    o_ref[0] = _rows_to_heads(o, NSA_HEADS)
    score_t = _dot_x3(jnp.sum(p3, axis=0), m_ref[...]).T
    nsb = score_t.shape[0]
    blk = lax.broadcasted_iota(jnp.int32, (nsb, tq), 0)
    qp = i * tq + lax.broadcasted_iota(jnp.int32, (nsb, tq), 1)
    cur = jnp.right_shift(qp, 6)
    valid = blk * SLC_BLOCK <= qp
    forced = (blk == 0) | (blk == cur) | (blk == cur - 1)
    sc = jnp.where(valid, jnp.where(forced, 1e30, score_t), -1e30)
    rank = jnp.zeros((nsb, tq), F32)
    for b in range(nsb):
        other = sc[b:b + 1, :]
        rank = rank + jnp.where((other > sc) | ((other == sc) & (blk > b)), 1.0, 0.0)
    sel_ref[0] = jnp.where((rank < SLC_TOPN) & valid, 1.0, 0.0)


def _nsa_cmp(qn, comp, tq):
    b, t, _ = qn.shape
    nc = comp.shape[1]
    nsb = t // SLC_BLOCK
    return pl.pallas_call(
        functools.partial(_nsa_cmp_kernel, tq=tq), grid=(b, t // tq),
        in_specs=[pl.BlockSpec((1, tq, 512), lambda bi, i: (bi, i, 0)),
                  pl.BlockSpec((1, nc, 128), lambda bi, i: (bi, 0, 0)),
                  pl.BlockSpec((nc, nsb), lambda bi, i: (0, 0))],
        out_specs=[pl.BlockSpec((1, tq, 512), lambda bi, i: (bi, i, 0)),
                   pl.BlockSpec((1, nsb, tq), lambda bi, i: (bi, 0, i))],
        out_shape=[jax.ShapeDtypeStruct((b, t, 512), F32), jax.ShapeDtypeStruct((b, nsb, t), F32)],
        compiler_params=_cparams("parallel", "parallel"), name="nsa_cmp")(qn, comp, _score_matrix(nc, nsb))


def _gate_expand(c):
    g = np.zeros((128, 512), np.float32)
    for h in range(NSA_HEADS):
        g[3 * h + c, NSA_HD * h:NSA_HD * (h + 1)] = 1.0
    return g


def _gate_mats():
    return jnp.asarray(np.stack([_gate_expand(c) for c in range(3)]), dtype=BF16)


def _nsa_combine(o_cmp, o_slc, o_win, gate, gx_ref, z):
    g0 = _dot_x3(gate, gx_ref[0])
    g1 = _dot_x3(gate, gx_ref[1])
    g2 = _dot_x3(gate, gx_ref[2])
    return ((g0 * o_cmp + g1 * o_slc + g2 * o_win) * _silu(z)).astype(BF16)


def _flash_reset(m_s, l_s, acc_s):
    m_s[...] = jnp.full_like(m_s, NEG)
    l_s[...] = jnp.zeros_like(l_s)
    acc_s[...] = jnp.zeros_like(acc_s)


def _flash_update(s_t, v, m_s, l_s, acc_s):
    m_new = jnp.maximum(m_s[...], jnp.max(s_t, axis=0, keepdims=True))
    alpha = jnp.exp(m_s[...] - m_new)
    p = jnp.exp(s_t - m_new)
    l_s[...] = alpha * l_s[...] + jnp.sum(p, axis=0, keepdims=True)
    acc_s[...] = alpha * acc_s[...] + _dot_tn(v, p)
    m_s[...] = m_new


def _nsa_attn_kernel(qr_ref, sel_ref, rows_ref, win_ref, ocmp_ref, gate_ref, gx_ref, z_ref, out_ref,
                     m_s, l_s, acc_s, *, tq, tk):
    i = pl.program_id(1)
    q = qr_ref[0] * NSA_SCALE
    q_t = jnp.concatenate([q[:, NSA_HD * h:NSA_HD * (h + 1)].T for h in range(NSA_HEADS)], axis=1).astype(BF16)
    sel_t = sel_ref[0].astype(BF16)
    nsb = sel_t.shape[0]
    koff = lax.broadcasted_iota(jnp.int32, (tq, tq), 0)
    qpos = i * tq + lax.broadcasted_iota(jnp.int32, (tq, tq), 1)
    stats = (m_s, l_s, acc_s)

    def scores(k, mask):
        bias = jnp.where(mask, 0.0, NEG)
        return _dot(k, q_t) + jnp.concatenate([bias] * NSA_HEADS, axis=1)

    def finish():
        o_t = acc_s[...] / l_s[...]
        return jnp.concatenate([o_t[:, tq * h:tq * (h + 1)].T for h in range(NSA_HEADS)], axis=1)

    def slc_tile(j, carry):
        start = pl.multiple_of(j * tk, tk)
        k = rows_ref[0, pl.ds(start, tk), 128:192]
        v = rows_ref[0, pl.ds(start, tk), 192:256]
        kpos = start + lax.broadcasted_iota(jnp.int32, (tk, tq), 0)
        expand = (lax.broadcasted_iota(jnp.int32, (tk, nsb), 1)
                  == jnp.right_shift(start + lax.broadcasted_iota(jnp.int32, (tk, nsb), 0), 6))
        selk = jnp.dot(jnp.where(expand, 1.0, 0.0).astype(BF16), sel_t, preferred_element_type=F32)
        visible = kpos <= i * tq + lax.broadcasted_iota(jnp.int32, (tk, tq), 1)
        _flash_update(scores(k, visible & (selk > 0.5)), v, *stats)
        return carry

    _flash_reset(*stats)
    lax.fori_loop(0, (i * tq + tq + tk - 1) // tk, slc_tile, 0)
    o_slc = finish()

    def win_tile(n, carry):
        start = pl.multiple_of((i - n) * tq, tq)
        k = win_ref[0, pl.ds(start, tq), 0:64]
        v = win_ref[0, pl.ds(start, tq), 64:128]
        kpos = start + koff
        _flash_update(scores(k, (kpos <= qpos) & (kpos > qpos - WINDOW)), v, *stats)
        return carry

    _flash_reset(*stats)
    lax.fori_loop(0, jnp.minimum(i, WINDOW // tq) + 1, win_tile, 0)
    o_win = finish()
    out_ref[0] = _nsa_combine(ocmp_ref[0], o_slc, o_win, gate_ref[0], gx_ref, z_ref[0])


def _nsa_attn(qr, sel_t, rows, win, o_cmp, gate, h, tq):
    b, t, _ = qr.shape
    nsb = sel_t.shape[1]
    tile = lambda w: pl.BlockSpec((1, tq, w), lambda bi, i: (bi, i, 0))
    return pl.pallas_call(
        functools.partial(_nsa_attn_kernel, tq=tq, tk=4 * tq), grid=(b, t // tq),
        in_specs=[tile(512), pl.BlockSpec((1, nsb, tq), lambda bi, i: (bi, 0, i)),
                  pl.BlockSpec((1, t, 256), lambda bi, i: (bi, 0, 0)),
                  pl.BlockSpec((1, t, 128), lambda bi, i: (bi, 0, 0)),
                  tile(512), tile(128),
                  pl.BlockSpec((3, 128, 512), lambda bi, i: (0, 0, 0)),
                  pl.BlockSpec((1, tq, 512), lambda bi, i: (bi, i, 2))],
        out_specs=tile(512),
        out_shape=jax.ShapeDtypeStruct((b, t, 512), BF16),
        scratch_shapes=[pltpu.VMEM((1, NSA_HEADS * tq), F32), pltpu.VMEM((1, NSA_HEADS * tq), F32),
                        pltpu.VMEM((NSA_HD, NSA_HEADS * tq), F32)],
        compiler_params=_cparams("parallel", "arbitrary"), name="nsa_attn",
    )(qr, sel_t, rows, win, o_cmp, gate, _gate_mats(), h)


def _page_stream(pt_ref, buf, sem, n_groups, group, page_src, compute):
    slots = buf.shape[0]
    total = pt_ref.shape[0] * n_groups
    assert total >= slots

    def copies(it):
        b = it // n_groups
        g = it - b * n_groups
        slot = it % slots
        return [pltpu.make_async_copy(page_src(pt_ref[b, g * group + k]), buf.at[slot, k], sem.at[slot])
                for k in range(group)]

    for it0 in range(slots - 1):
        for cp in copies(it0):
            cp.start()

    def body(it, carry):
        for cp in copies(it):
            cp.wait()

        @pl.when(it + (slots - 1) < total)
        def _():
            for cp in copies(it + (slots - 1)):
                cp.start()

        b = it // n_groups
        compute(b, it - b * n_groups, it % slots)
        return carry

    lax.fori_loop(0, total, body, 0)


def _nsa_cmp_sample_kernel(pt_ref, cache_ref, qn_ref, wt_ref, g8_ref, m_ref, o_ref, sel_ref,
                           buf, sem, sums_s, *, layer, n_pages, past_len):
    per_page = PAGE_SIZE // CMP_STRIDE
    group = buf.shape[1]
    n_groups = n_pages // group
    nc = n_pages * per_page
    sums_s[pl.ds(nc, 8), :] = jnp.zeros((8, 4 * NSA_HD), F32)

    def compute(b, g, slot):
        base = g * (group * per_page)
        for k in range(group):
            k_t = buf[slot, k, 0]
            v_t = buf[slot, k, 1]
            stacked = jnp.concatenate([(k_t * wt_ref[0]).astype(BF16), (k_t * wt_ref[1]).astype(BF16),
                                       (v_t * wt_ref[2]).astype(BF16), (v_t * wt_ref[3]).astype(BF16)], axis=0)
            off = pl.multiple_of(base + k * per_page, per_page)
            sums_s[pl.ds(off, per_page), :] = lax.dot_general(
                g8_ref[...], stacked, (((1,), (1,)), ((), ())), preferred_element_type=F32)

        @pl.when(g == n_groups - 1)
        def _():
            _nsa_cmp_sample_finish(b, qn_ref, m_ref, o_ref, sel_ref, sums_s, nc=nc, past_len=past_len)

    _page_stream(pt_ref, buf, sem, n_groups, group, lambda page: cache_ref.at[layer, page, pl.ds(0, 2)], compute)


def _nsa_cmp_sample_finish(b, qn_ref, m_ref, o_ref, sel_ref, sums_s, *, nc, past_len):
    first = sums_s[pl.ds(0, nc), :]
    second = sums_s[pl.ds(1, nc), :]
    ck = first[:, 0:64] + second[:, 64:128]
    cv = first[:, 128:192] + second[:, 192:256]
    q = qn_ref[b]
    s = _dot_nt(q, ck) * NSA_SCALE
    blk_end = lax.broadcasted_iota(jnp.int32, (NSA_HEADS, nc), 1) * CMP_STRIDE + (CMP_BLOCK - 1)
    mask = blk_end <= past_len
    s = jnp.where(mask, s, NEG)
    e = jnp.where(mask, jnp.exp(s - jnp.max(s, axis=-1, keepdims=True)), 0.0)
    p = e / jnp.maximum(jnp.sum(e, axis=-1, keepdims=True), 1e-30)
    o_ref[b] = _dot(p, cv)
    imp = jnp.sum(p, axis=0, keepdims=True)
    score = _dot_x3(imp, m_ref[...])
    nsb = score.shape[1]
    lane = lax.broadcasted_iota(jnp.int32, (nsb, nsb), 1)
    sub = lax.broadcasted_iota(jnp.int32, (nsb, nsb), 0)
    forced_row = (lane == 0) | (lane == nsb - 1)
    sc_row = jnp.where(forced_row, 1e30, jnp.broadcast_to(score, (nsb, nsb)))
    eye = lane == sub
    sc_col = jnp.sum(jnp.where(eye, sc_row, 0.0), axis=1, keepdims=True)
    beats_row = (sc_col > sc_row) | ((sc_col == sc_row) & (sub < lane))
    rank_row = jnp.sum(jnp.where(beats_row, 1.0, 0.0), axis=0, keepdims=True)
    sel_row = rank_row < (SLC_TOPN - 1)
    beats_col = (sc_row > sc_col) | ((sc_row == sc_col) & (lane < sub))
    rank_col = jnp.sum(jnp.where(beats_col, 1.0, 0.0), axis=1, keepdims=True)
    sel_col = rank_col < (SLC_TOPN - 1)
    pos_col = jnp.sum(jnp.where(sel_row & (lane < sub), 1.0, 0.0), axis=1, keepdims=True)
    slot = lax.broadcasted_iota(jnp.int32, (nsb, 128), 1).astype(F32)
    blk_id = lax.broadcasted_iota(jnp.int32, (nsb, 128), 0).astype(F32)
    onehot = sel_col & (pos_col == slot)
    sel_ref[b] = jnp.sum(jnp.where(onehot, blk_id, 0.0), axis=0, keepdims=True).astype(jnp.int32)


def _nsa_cmp_sample(cache_kv_t, layer, page_table, qn, w_pos, pages_per_step):
    s, n_pages = page_table.shape
    past_len = n_pages * PAGE_SIZE
    per_page = PAGE_SIZE // CMP_STRIDE
    nc = n_pages * per_page
    nsb = past_len // SLC_BLOCK
    group = pages_per_step
    assert n_pages % group == 0
    tile_t = lambda w: jnp.tile(w.T, (1, per_page))
    wt = jnp.stack([tile_t(w_pos[0, :CMP_STRIDE]), tile_t(w_pos[0, CMP_STRIDE:]),
                    tile_t(w_pos[1, :CMP_STRIDE]), tile_t(w_pos[1, CMP_STRIDE:])])
    g8 = jnp.asarray((np.arange(per_page)[:, None] == (np.arange(PAGE_SIZE) // CMP_STRIDE)[None, :])
                     .astype(np.float32), dtype=BF16)
    vmem = pl.BlockSpec(memory_space=pltpu.VMEM)
    return pl.pallas_call(
        functools.partial(_nsa_cmp_sample_kernel, layer=layer, n_pages=n_pages, past_len=past_len),
        in_specs=[pl.BlockSpec(memory_space=pltpu.SMEM), pl.BlockSpec(memory_space=pl.ANY),
                  vmem, vmem, vmem, vmem],
        out_specs=[vmem, vmem],
        out_shape=[jax.ShapeDtypeStruct((s, NSA_HEADS, NSA_HD), F32),
                   jax.ShapeDtypeStruct((s, 1, 128), jnp.int32)],
        scratch_shapes=[pltpu.VMEM((CMP_STREAM_SLOTS, group, 2, NSA_HD, PAGE_SIZE), F32),
                        pltpu.SemaphoreType.DMA((CMP_STREAM_SLOTS,)),
                        pltpu.VMEM((nc + 8, 4 * NSA_HD), F32)],
        compiler_params=pltpu.CompilerParams(vmem_limit_bytes=VMEM_LIMIT), name="nsa_cmp_sample",
    )(page_table, cache_kv_t, qn, wt, g8, _score_matrix(nc, nsb))


def _attend_with_new(q, s_past, v_t_parts, k_new, v_new):
    s_new = jnp.sum(q * k_new, axis=-1, keepdims=True)
    m = jnp.maximum(jnp.max(s_past, axis=-1, keepdims=True), s_new)
    p = jnp.exp(s_past - m)
    p_new = jnp.exp(s_new - m)
    acc = p_new * v_new
    off = 0
    for v_t in v_t_parts:
        n = v_t.shape[1]
        acc = acc + _dot_nt(p[:, off:off + n], v_t)
        off += n
    return acc / (jnp.sum(p, axis=-1, keepdims=True) + p_new)


def _nsa_slc_sample_kernel(sel_ref, pt_ref, *refs, n_sel, past_len):
    b = pl.program_id(0)
    blk_refs = refs[:n_sel]
    qr_ref, new_ref, winbuf_ref, oslc_ref, owin_ref, newwin_ref = refs[n_sel:]
    q = qr_ref[0] * NSA_SCALE
    new = new_ref[0]
    per_page = PAGE_SIZE // SLC_BLOCK
    lane = lax.broadcasted_iota(jnp.int32, (NSA_HEADS, PAGE_SIZE), 1)
    parts = []
    for k, r in enumerate(blk_refs):
        half = sel_ref[b, k] % per_page
        parts.append(jnp.where(jnp.right_shift(lane, 6) == half, _dot(q, r[0, 0, 0]), NEG))
    oslc_ref[0] = _attend_with_new(q, jnp.concatenate(parts, axis=1), [r[0, 0, 1] for r in blk_refs],
                                   new[:, 128:192], new[:, 192:256])
    lw = winbuf_ref.shape[4]
    kw_t = winbuf_ref[0, 0, 0]
    vw_t = winbuf_ref[0, 0, 1]
    k_pos = past_len - lw + lax.broadcasted_iota(jnp.int32, (NSA_HEADS, lw), 1)
    s = jnp.where(k_pos > past_len - WINDOW, _dot(q, kw_t), NEG)
    owin_ref[0] = _attend_with_new(q, s, [vw_t], new[:, 256:320], new[:, 320:384])
    eye = (lax.broadcasted_iota(jnp.int32, (NSA_HD, NSA_HD), 0)
           == lax.broadcasted_iota(jnp.int32, (NSA_HD, NSA_HD), 1))
    to_col = lambda r: jnp.sum(jnp.where(eye, r, 0.0), axis=1, keepdims=True)
    last = lax.broadcasted_iota(jnp.int32, (NSA_HD, lw), 1) == lw - 1
    newwin_ref[0, 0, 0] = jnp.where(last, to_col(new[:, 256:320]), pltpu.roll(kw_t, lw - 1, 1))
    newwin_ref[0, 0, 1] = jnp.where(last, to_col(new[:, 320:384]), pltpu.roll(vw_t, lw - 1, 1))


def _nsa_slc_sample(cache_kv_t, cache_win_t, layer, page_table, sel, qr, kvr):
    s, n_pages = page_table.shape
    past_len = n_pages * PAGE_SIZE
    lw = cache_win_t.shape[4]
    n_sel = SLC_TOPN - 1
    per_page = PAGE_SIZE // SLC_BLOCK

    def blk_spec(k):
        return pl.BlockSpec((1, 1, 2, NSA_HD, PAGE_SIZE),
                            lambda b, sl, pt: (layer, pt[b, sl[b, k] // per_page], 1, 0, 0))

    grid_spec = pltpu.PrefetchScalarGridSpec(
        num_scalar_prefetch=2, grid=(s,),
        in_specs=[blk_spec(k) for k in range(n_sel)]
        + [pl.BlockSpec((1, NSA_HEADS, NSA_HD), lambda b, sl, pt: (b, 0, 0)),
           pl.BlockSpec((1, 1, 384), lambda b, sl, pt: (b, 0, 0)),
           pl.BlockSpec((1, 1, 2, NSA_HD, lw), lambda b, sl, pt: (layer, b, 0, 0, 0))],
        out_specs=[pl.BlockSpec((1, NSA_HEADS, NSA_HD), lambda b, sl, pt: (b, 0, 0)),
                   pl.BlockSpec((1, NSA_HEADS, NSA_HD), lambda b, sl, pt: (b, 0, 0)),
                   pl.BlockSpec((1, 1, 2, NSA_HD, lw), lambda b, sl, pt: (0, b, 0, 0, 0))])
    return pl.pallas_call(
        functools.partial(_nsa_slc_sample_kernel, n_sel=n_sel, past_len=past_len),
        grid_spec=grid_spec,
        out_shape=[jax.ShapeDtypeStruct((s, NSA_HEADS, NSA_HD), F32),
                   jax.ShapeDtypeStruct((s, NSA_HEADS, NSA_HD), F32),
                   jax.ShapeDtypeStruct((1, s, 2, NSA_HD, lw), F32)],
        compiler_params=_cparams("parallel"), name="nsa_slc_sample",
    )(sel, page_table, *([cache_kv_t] * n_sel), qr, kvr, cache_win_t)


def _nsa_post_sample_kernel(ocmp_ref, oslc_ref, owin_ref, gate_ref, gx_ref, z_ref, out_ref):
    out_ref[...] = _nsa_combine(ocmp_ref[...], oslc_ref[...], owin_ref[...], gate_ref[...], gx_ref, z_ref[...])


def _nsa_post_sample(o_cmp, o_slc, o_win, gate, z):
    s = gate.shape[0]
    return pl.pallas_call(
        _nsa_post_sample_kernel, out_shape=jax.ShapeDtypeStruct((s, 512), BF16), name="nsa_post_sample",
    )(o_cmp, o_slc, o_win, gate, _gate_mats(), z)


def _lru_gates(u, wa_ref, ba_ref, wx_ref, bx_ref, lam_ref):
    r =jax.nn.sigmoid(_dot(u, wa_ref[...]) + ba_ref[...])
    ig = jax.nn.sigmoid(_dot(u, wx_ref[...]) + bx_ref[...])
    log_a = -LRU_C * r * jax.nn.softplus(-lam_ref[...])
    a = jnp.exp(log_a)
    b = jnp.sqrt(jnp.tanh(-log_a) * (a * a + 1.0)) * (ig * u)
    return a, b


def _lru_kernel(h_ref, cw_ref, cb_ref, wa_ref, ba_ref, wx_ref, bx_ref, lam_ref,
                br_ref, hlast_ref, conv_ref, xbuf, hc):
    t = pl.program_id(1)
    tt = h_ref.shape[1]

    @pl.when(t == 0)
    def _():
        xbuf[0:8, :] = jnp.zeros((8, BRANCH_W), F32)
        hc[...] = jnp.zeros_like(hc)

    x = h_ref[0, :, 0:512]
    z = h_ref[0, :, 512:1024]
    xbuf[8:8 + tt, :] = x
    u = cb_ref[...] + xbuf[pl.ds(8, tt), :] * cw_ref[3:4, :]
    for j in range(CONV_W - 1):
        u = u + xbuf[pl.ds(5 + j, tt), :] * cw_ref[j:j + 1, :]
    xbuf[0:8, :] = xbuf[tt:tt + 8, :]
    a, b = _lru_gates(u, wa_ref, ba_ref, wx_ref, bx_ref, lam_ref)
    row = lax.broadcasted_iota(jnp.int32, (tt, BRANCH_W), 0)
    d = 1
    while d < tt:
        keep = row >= d
        b = jnp.where(keep, a * pltpu.roll(b, d, 0) + b, b)
        a = jnp.where(keep, a * pltpu.roll(a, d, 0), a)
        d *= 2
    hseq = a * hc[...] + b
    hc[...] = hseq[tt - 1:tt, :]
    br_ref[0] = (hseq * _silu(z)).astype(BF16)

    @pl.when(t == pl.num_programs(1) - 1)
    def _():
        hlast_ref[0] = hseq[tt - 1:tt, :]
        conv_ref[0] = x[tt - (CONV_W - 1):tt, :]


def _block_diag(w):
    n, c, d = w.shape
    eye = jnp.eye(n, dtype=w.dtype)
    return (w[:, :, None, :] * eye[:, None, :, None]).reshape(n * c, n * d)


def _lru_weights(p):
    row = lambda v: v.reshape(1, BRANCH_W)
    return (p["lru_conv_w"], row(p["lru_conv_b"]), _block_diag(p["lru_wa"]).astype(BF16), row(p["lru_ba"]),
            _block_diag(p["lru_wx"]).astype(BF16), row(p["lru_bx"]), row(p["lru_lam"]))


def _lru_prompt(h, p, tt):
    b, t, _ = h.shape
    const = lambda r, w: pl.BlockSpec((r, w), lambda bi, i: (0, 0))
    return pl.pallas_call(
        _lru_kernel, grid=(b, t // tt),
        in_specs=[pl.BlockSpec((1, tt, SEG_LRU), lambda bi, i: (bi, i, 0)),
                  const(CONV_W, 512), const(1, 512), const(512, 512), const(1, 512), const(512, 512),
                  const(1, 512), const(1, 512)],
        out_specs=[pl.BlockSpec((1, tt, 512), lambda bi, i: (bi, i, 0)),
                   pl.BlockSpec((1, 1, 512), lambda bi, i: (bi, 0, 0)),
                   pl.BlockSpec((1, CONV_W - 1, 512), lambda bi, i: (bi, 0, 0))],
        out_shape=[jax.ShapeDtypeStruct((b, t, 512), BF16), jax.ShapeDtypeStruct((b, 1, 512), F32),
                   jax.ShapeDtypeStruct((b, CONV_W - 1, 512), F32)],
        scratch_shapes=[pltpu.VMEM((tt + 8, 512), F32), pltpu.VMEM((1, 512), F32)],
        compiler_params=_cparams("parallel", "arbitrary"), name="lru_prompt")(h, *_lru_weights(p))


def _lru_sample_kernel(h_ref, c0_ref, c1_ref, c2_ref, h0_ref, cw_ref, cb_ref, wa_ref, ba_ref, wx_ref, bx_ref,
                       lam_ref, br_ref, hnew_ref):
    x = h_ref[:, 0:512]
    z = h_ref[:, 512:1024]
    u = cb_ref[...] + x * cw_ref[3:4, :]
    for j, c_ref in enumerate((c0_ref, c1_ref, c2_ref)):
        u = u + c_ref[...] * cw_ref[j:j + 1, :]
    a, b = _lru_gates(u, wa_ref, ba_ref, wx_ref, bx_ref, lam_ref)
    hnew = a * h0_ref[...] + b
    hnew_ref[...] = hnew
    br_ref[...] = (hnew * _silu(z)).astype(BF16)


def _lru_sample(h, conv_prev_t, h0, p):
    s = h.shape[0]
    c0, c1, c2 = (conv_prev_t[j] for j in range(CONV_W - 1))
    return pl.pallas_call(
        _lru_sample_kernel,
        out_shape=[jax.ShapeDtypeStruct((s, 512), BF16), jax.ShapeDtypeStruct((s, 512), F32)],
        name="lru_sample")(h, c0, c1, c2, h0, *_lru_weights(p))


def _gla_constants(c):
    tri = np.tril(np.ones((c, c), np.float32))
    rows = [tri]
    masks = []
    t = np.arange(c)
    b = c // 2
    while b >= 1:
        ref_row = (t // (2 * b)) * 2 * b + b - 1
        rows.append(tri[ref_row])
        same = (t[:, None] // (2 * b)) == (t[None, :] // (2 * b))
        masks.append((same & ((t[:, None] // b) % 2 == 1) & ((t[None, :] // b) % 2 == 0)).astype(np.float32))
        b //= 2
    return (jnp.asarray(np.concatenate(rows, axis=0), dtype=BF16), jnp.asarray(np.stack(masks), dtype=F32))


def _gla_log_alpha(a_in, wa_ref, ba_ref):
    return jax.nn.log_sigmoid(_dot(a_in, wa_ref[...]) + ba_ref[...]) * (1.0 / GLA_TAU)


def _gla_out(o, gout_ref, z):
    outs = []
    for h in range(GLA_HEADS):
        oh = o[h]
        ms = jnp.mean(oh * oh, axis=-1, keepdims=True)
        outs.append(oh * lax.rsqrt(ms + NORM_EPS) * gout_ref[...])
    return (jnp.concatenate(outs, axis=1) * _silu(z)).astype(BF16)


def _gla_kernel(h_ref, wa_ref, ba_ref, gout_ref, cum_ref, lm_ref, br_ref, state_ref, st_s):
    ci = pl.program_id(1)
    c = h_ref.shape[1]
    n_lvl = lm_ref.shape[0]

    @pl.when(ci == 0)
    def _():
        st_s[...] = jnp.zeros_like(st_s)

    h = h_ref[0]
    q = h[:, 0:256] * (GLA_DK ** -0.5)
    k = h[:, 256:512]
    v = h[:, 512:1024]
    g = _gla_log_alpha(h[:, 1024:1152], wa_ref, ba_ref)
    cums = _x3_dot(cum_ref[...], g)
    bc = cums[0:c]
    b_end = bc[c - 1:c, :]
    q_in = q * jnp.exp(bc)
    k_end = k * jnp.exp(b_end - bc)
    aq = []
    ak = []
    for lv in range(n_lvl):
        ref = cums[(lv + 1) * c:(lv + 2) * c]
        aq.append(q * jnp.exp(jnp.minimum(bc - ref, 0.0)))
        ak.append(k * jnp.exp(jnp.minimum(ref - bc, 0.0)))
    eye = (lax.broadcasted_iota(jnp.int32, (c, c), 0) == lax.broadcasted_iota(jnp.int32, (c, c), 1))
    decay_end = jnp.exp(b_end)
    outs = []
    for hd in range(GLA_HEADS):
        ks = slice(GLA_DK * hd, GLA_DK * (hd + 1))
        vh = v[:, GLA_DV * hd:GLA_DV * (hd + 1)]
        att = jnp.where(eye, _dot_nt(q[:, ks], k[:, ks]), 0.0)
        for lv in range(n_lvl):
            att = att + lm_ref[lv] * _dot_nt(aq[lv][:, ks], ak[lv][:, ks])
        st = st_s[hd]
        outs.append(_dot_nt(q_in[:, ks], st) + _dot(att, vh))
        st_s[hd] = st * decay_end[:, ks] + _dot_tn(vh, k_end[:, ks])
    br_ref[0] = _gla_out(outs, gout_ref, h[:, 1152:1664])

    @pl.when(ci == pl.num_programs(1) - 1)
    def _():
        for hd in range(GLA_HEADS):
            state_ref[0, hd] = st_s[hd].T


def _gla_weights(p):
    wa = jnp.zeros((128, GLA_HEADS * GLA_DK), F32).at[:GLA_RANK].set(p["gla_w_a2"]).astype(BF16)
    return wa, p["gla_b_a"].reshape(1, -1), p["gla_out_g"].reshape(1, -1)


def _gla_prompt(h, p):
    b, t, _ = h.shape
    c = GLA_CHUNK
    cum, lm = _gla_constants(c)
    const2 = lambda r, w: pl.BlockSpec((r, w), lambda bi, i: (0, 0))
    return pl.pallas_call(
        _gla_kernel, grid=(b, t // c),
        in_specs=[pl.BlockSpec((1, c, SEG_GLA), lambda bi, i: (bi, i, 0)),
                  const2(128, 256), const2(1, 256), const2(1, GLA_DV), const2(cum.shape[0], c),
                  pl.BlockSpec(lm.shape, lambda bi, i: (0, 0, 0))],
        out_specs=[pl.BlockSpec((1, c, 512), lambda bi, i: (bi, i, 0)),
                   pl.BlockSpec((1, GLA_HEADS, GLA_DK, GLA_DV), lambda bi, i: (bi, 0, 0, 0))],
        out_shape=[jax.ShapeDtypeStruct((b, t, 512), BF16),
                   jax.ShapeDtypeStruct((b, GLA_HEADS, GLA_DK, GLA_DV), F32)],
        scratch_shapes=[pltpu.VMEM((GLA_HEADS, GLA_DV, GLA_DK), F32)],
        compiler_params=_cparams("parallel", "arbitrary"), name="gla_prompt")(h, *_gla_weights(p), cum, lm)


def _gla_sample_kernel(h_ref, s0_ref, wa_ref, ba_ref, gout_ref, br_ref, snew_ref):
    eye = (lax.broadcasted_iota(jnp.int32, (GLA_DK, GLA_DK), 0)
           == lax.broadcasted_iota(jnp.int32, (GLA_DK, GLA_DK), 1))
    to_col = lambda r: jnp.sum(jnp.where(eye, r, 0.0), axis=1, keepdims=True)
    for i in range(h_ref.shape[0]):
        h = h_ref[i]
        q = h[:, 0:256] * (GLA_DK ** -0.5)
        k = h[:, 256:512]
        v = h[:, 512:1024]
        g = _gla_log_alpha(h[:, 1024:1152], wa_ref, ba_ref)
        decay = jnp.exp(g)
        outs = []
        for hd in range(GLA_HEADS):
            ks = slice(GLA_DK * hd, GLA_DK * (hd + 1))
            vh = v[:, GLA_DV * hd:GLA_DV * (hd + 1)]
            s0 = s0_ref[0, i, hd]
            qk = jnp.sum(q[:, ks] * k[:, ks], axis=-1, keepdims=True)
            outs.append(_dot(q[:, ks] * decay[:, ks], s0) + qk * vh)
            snew_ref[i, hd] = to_col(decay[:, ks]) * s0 + to_col(k[:, ks]) * vh
        br_ref[i] = _gla_out(outs, gout_ref, h[:, 1152:1664])


def _gla_sample(h, state, layer, p):
    s = h.shape[0]
    per = GLA_SAMPLE_SEQS if s % GLA_SAMPLE_SEQS == 0 else 1
    const2 = lambda r, w: pl.BlockSpec((r, w), lambda b: (0, 0))
    br, snew = pl.pallas_call(
        _gla_sample_kernel, grid=(s // per,),
        in_specs=[pl.BlockSpec((per, 1, SEG_GLA), lambda b: (b, 0, 0)),
                  pl.BlockSpec((1, per, GLA_HEADS, GLA_DK, GLA_DV), lambda b: (layer, b, 0, 0, 0)),
                  const2(128, 256), const2(1, 256), const2(1, GLA_DV)],
        out_specs=[pl.BlockSpec((per, 1, 512), lambda b: (b, 0, 0)),
                   pl.BlockSpec((per, GLA_HEADS, GLA_DK, GLA_DV), lambda b: (b, 0, 0, 0))],
        out_shape=[jax.ShapeDtypeStruct((s, 1, 512), BF16),
                   jax.ShapeDtypeStruct((s, GLA_HEADS, GLA_DK, GLA_DV), F32)],
        compiler_params=_cparams("parallel"), name="gla_sample",
    )(h.reshape(s, 1, SEG_GLA), state, *_gla_weights(p))
    return br.reshape(s, 512), snew


def _mla_prep_kernel(h_ref, cq_g_ref, wuq_ref, grp_ref, qg_ref, wuk_ref, ckv_g_ref, kr_g_ref,
                     cosq_ref, sinq_ref, cosk_ref, sink_ref, qcat_ref, lat_ref):
    h = h_ref[0]
    cq = h[:, 512:896]
    cqn =cq * lax.rsqrt(jnp.mean(cq * cq, axis=-1, keepdims=True) + NORM_EPS) * cq_g_ref[...]
    q = _dot(cqn, wuq_ref[...])
    ms = _dot_x3(q * q, grp_ref[...]) * (1.0 / (MLA_NOPE + MLA_ROPE))
    qn = q * lax.rsqrt(ms + NORM_EPS) * qg_ref[...]
    q_rope = _rope64(qn[:, 512:768], cosq_ref[...], sinq_ref[...])
    for hd in range(MLA_HEADS):
        q_lat = _dot(qn[:, MLA_NOPE * hd:MLA_NOPE * (hd + 1)], wuk_ref[hd])
        qcat_ref[0, hd] = jnp.concatenate(
            [q_lat.T, q_rope[:, MLA_ROPE * hd:MLA_ROPE * (hd + 1)].T], axis=0).astype(BF16)
    ckv = h[:, 896:1024]
    c =ckv * lax.rsqrt(jnp.mean(ckv * ckv, axis=-1, keepdims=True) + NORM_EPS) * ckv_g_ref[...]
    kr = h[:, 1024:1152]
    ms_kr = jnp.sum(kr * kr, axis=-1, keepdims=True) * (1.0 / MLA_ROPE)
    krn = kr * lax.rsqrt(ms_kr + NORM_EPS) * kr_g_ref[...]
    krr = _rope64(krn, cosk_ref[...], sink_ref[...])
    lat_ref[0] = jnp.concatenate([c, krr[:, 0:MLA_ROPE]], axis=1)


def _mla_weights(p):
    hw = MLA_NOPE + MLA_ROPE
    w = p["mla_w_uq"].reshape(MLA_Q_LORA, MLA_HEADS, hw)
    wuq = jnp.concatenate([w[:, :, :MLA_NOPE].reshape(MLA_Q_LORA, -1),
                           w[:, :, MLA_NOPE:].reshape(MLA_Q_LORA, -1)], axis=1).astype(BF16)
    qg = jnp.concatenate([jnp.tile(p["mla_q_g"][:MLA_NOPE], MLA_HEADS),
                          jnp.tile(p["mla_q_g"][MLA_NOPE:], MLA_HEADS)]).reshape(1, -1)
    head = np.concatenate([np.arange(MLA_HEADS * MLA_NOPE) // MLA_NOPE, np.arange(MLA_HEADS * MLA_ROPE) // MLA_ROPE])
    grp = jnp.asarray((head[:, None] == head[None, :]).astype(np.float32), dtype=BF16)
    wuk_t = jnp.swapaxes(p["mla_w_uk"], 1, 2).astype(BF16)
    kr_g = jnp.concatenate([p["mla_kr_g"], jnp.zeros((128 - MLA_ROPE,), F32)]).reshape(1, 128)
    return (p["mla_cq_g"].reshape(1, -1), wuq, grp, qg, wuk_t, p["mla_ckv_g"].reshape(1, -1), kr_g)


def _mla_prep(h, tabs, p, tq):
    b, t, _ = h.shape
    tab = lambda w: pl.BlockSpec((tq, w), lambda bi, i: (i, 0))
    const2 = lambda r, w: pl.BlockSpec((r, w), lambda bi, i: (0, 0))
    return pl.pallas_call(
        _mla_prep_kernel, grid=(b, t // tq),
        in_specs=[pl.BlockSpec((1, tq, SEG_MLA), lambda bi, i: (bi, i, 0)),
                  const2(1, 384), const2(384, 768), const2(768, 768), const2(1, 768),
                  pl.BlockSpec((MLA_HEADS, MLA_NOPE, MLA_KV_LORA), lambda bi, i: (0, 0, 0)),
                  const2(1, 128), const2(1, 128), tab(256), tab(256), tab(128), tab(128)],
        out_specs=[pl.BlockSpec((1, MLA_HEADS, MLA_ROW, tq), lambda bi, i: (bi, 0, 0, i)),
                   pl.BlockSpec((1, tq, MLA_ROW), lambda bi, i: (bi, i, 0))],
        out_shape=[jax.ShapeDtypeStruct((b, MLA_HEADS, MLA_ROW, t), BF16),
                   jax.ShapeDtypeStruct((b, t, MLA_ROW), F32)],
        compiler_params=_cparams("parallel", "parallel"), name="mla_prep")(h, *_mla_weights(p), *tabs)


def _mla_up(o_lat, wuv_ref, z):
    outs = [_dot(o_lat[hd], wuv_ref[hd]) for hd in range(MLA_HEADS)]
    return (jnp.concatenate(outs, axis=1) * _silu(z)).astype(BF16)


def _mla_attn_kernel(q_ref, lat_ref, wuv_ref, z_ref, out_ref, m_s, l_s, acc_s, *, tq):
    i = pl.program_id(1)
    q_t = jnp.concatenate([q_ref[0, hd] for hd in range(MLA_HEADS)], axis=1)
    stats = (m_s, l_s, acc_s)
    _flash_reset(*stats)

    def tile(j, bias):
        start = pl.multiple_of(j * tq, tq)
        lat = lat_ref[0, pl.ds(start, tq), :].astype(BF16)
        s_t = _dot(lat, q_t) * MLA_SCALE
        if bias is not None:
            s_t = s_t + jnp.concatenate([bias] * MLA_HEADS, axis=1)
        _flash_update(s_t, lat[:, 0:MLA_KV_LORA], *stats)

    def full_tile(j, carry):
        tile(j, None)
        return carry

    lax.fori_loop(0, i, full_tile, 0)
    causal = (lax.broadcasted_iota(jnp.int32, (tq, tq), 0) <= lax.broadcasted_iota(jnp.int32, (tq, tq), 1))
    tile(i, jnp.where(causal, 0.0, NEG))
    o_t = acc_s[...] / l_s[...]
    out_ref[0] = _mla_up([o_t[:, tq * hd:tq * (hd + 1)].T for hd in range(MLA_HEADS)], wuv_ref, z_ref[0])


def _mla_attn(qcat_t, lat, w_uv, h, tq):
    b, _, _, t = qcat_t.shape
    cols = MLA_HEADS * tq
    return pl.pallas_call(
        functools.partial(_mla_attn_kernel, tq=tq), grid=(b, t // tq),
        in_specs=[pl.BlockSpec((1, MLA_HEADS, MLA_ROW, tq), lambda bi, i: (bi, 0, 0, i)),
                  pl.BlockSpec((1, t, MLA_ROW), lambda bi, i: (bi, 0, 0)),
                  pl.BlockSpec((MLA_HEADS, MLA_KV_LORA, MLA_VHD), lambda bi, i: (0, 0, 0)),
                  pl.BlockSpec((1, tq, 512), lambda bi, i: (bi, i, 0))],
        out_specs=pl.BlockSpec((1, tq, 512), lambda bi, i: (bi, i, 0)),
        out_shape=jax.ShapeDtypeStruct((b, t, 512), BF16),
        scratch_shapes=[pltpu.VMEM((1, cols), F32), pltpu.VMEM((1, cols), F32),
                        pltpu.VMEM((MLA_KV_LORA, cols), F32)],
        compiler_params=_cparams("parallel", "arbitrary"), name="mla_attn")(qcat_t, lat, w_uv.astype(BF16), h)


def _mla_sample_kernel(pt_ref, cache_ref, q_ref, new_ref, o_ref, buf, sem, m_s, l_s, acc_s, *, layer, n_pages):
    group = buf.shape[1]
    n_groups = n_pages // group

    def compute(b, g, slot):
        q = q_ref[b]
        first = g == 0
        m_run = jnp.where(first, NEG, m_s[...])
        l_run = jnp.where(first, 0.0, l_s[...])
        acc_run = jnp.where(first, 0.0, acc_s[...])
        pages = [buf[slot, k].astype(BF16) for k in range(group)]
        scores = [_dot(q, pg) * MLA_SCALE for pg in pages]
        m_pg = [jnp.max(s, axis=-1, keepdims=True) for s in scores]
        probs = [jnp.exp(s - mk) for s, mk in zip(scores, m_pg)]
        l_pg = [jnp.sum(p, axis=-1, keepdims=True) for p in probs]
        o_pg = [_dot_nt(p, pg[0:MLA_KV_LORA, :]) for p, pg in zip(probs, pages)]
        m_new = functools.reduce(jnp.maximum, m_pg, m_run)
        w_run = jnp.exp(m_run - m_new)
        l_new = w_run * l_run
        acc = w_run * acc_run
        for mk, lk, ok in zip(m_pg, l_pg, o_pg):
            w = jnp.exp(mk - m_new)
            l_new = l_new + w * lk
            acc = acc + w * ok
        m_s[...] = m_new
        l_s[...] = l_new
        acc_s[...] = acc

        @pl.when(g == n_groups - 1)
        def _():
            new = new_ref[b]
            s_new = jnp.sum(q.astype(F32) * new, axis=-1, keepdims=True) * MLA_SCALE
            m_fin = jnp.maximum(m_new, s_new)
            a_fin = jnp.exp(m_new - m_fin)
            p_new = jnp.exp(s_new - m_fin)
            o_ref[b] = (a_fin * acc + p_new * new[:, 0:MLA_KV_LORA]) / (a_fin * l_new + p_new)

    m_s[...] = jnp.full_like(m_s, NEG)
    l_s[...] = jnp.zeros_like(l_s)
    acc_s[...] = jnp.zeros_like(acc_s)
    _page_stream(pt_ref, buf, sem, n_groups, group, lambda page: cache_ref.at[layer, page], compute)


def _mla_sample_attn(cache_mla_t, layer, page_table, q, new_rows, pages_per_step):
    s, n_pages = page_table.shape
    group = pages_per_step
    assert n_pages % group == 0
    vmem = pl.BlockSpec(memory_space=pltpu.VMEM)
    return pl.pallas_call(
        functools.partial(_mla_sample_kernel, layer=layer, n_pages=n_pages),
        in_specs=[pl.BlockSpec(memory_space=pltpu.SMEM), pl.BlockSpec(memory_space=pl.ANY), vmem, vmem],
        out_specs=vmem,
        out_shape=jax.ShapeDtypeStruct((s, MLA_HEADS, MLA_KV_LORA), F32),
        scratch_shapes=[pltpu.VMEM((MLA_STREAM_SLOTS, group, MLA_ROW, PAGE_SIZE), F32),
                        pltpu.SemaphoreType.DMA((MLA_STREAM_SLOTS,)),
                        pltpu.VMEM((MLA_HEADS, 1), F32), pltpu.VMEM((MLA_HEADS, 1), F32),
                        pltpu.VMEM((MLA_HEADS, MLA_KV_LORA), F32)],
        compiler_params=pltpu.CompilerParams(vmem_limit_bytes=VMEM_LIMIT), name="mla_sample_attn",
    )(page_table, cache_mla_t, q, new_rows)


def _mla_post_sample_kernel(o_ref, wuv_ref, z_ref, out_ref):
    out_ref[...] = _mla_up([o_ref[:, hd, :] for hd in range(MLA_HEADS)], wuv_ref, z_ref[...])


def _mla_post_sample(o_lat, w_uv, z):
    s = o_lat.shape[0]
    return pl.pallas_call(
        _mla_post_sample_kernel, out_shape=jax.ShapeDtypeStruct((s, 512), BF16), name="mla_post_sample",
    )(o_lat, w_uv.astype(BF16), z)


def _pad_cols(w, width):
    return jnp.pad(w, ((0, 0), (0, width - w.shape[1])))


def _input_weights(w_in):
    sizes = (512, 384, 24, 512, 512, 512, 256, 256, 512, 16, 512, 384, 128, 64, 512)
    offs = np.concatenate([[0], np.cumsum(sizes)])
    seg = [w_in[:, offs[i]:offs[i + 1]] for i in range(len(sizes))]
    (nsa_q, nsa_kv, nsa_g, nsa_z, lru_x, lru_z, gla_q, gla_k, gla_v, gla_a, gla_z,
     mla_cq, mla_ckv, mla_kr, mla_z) = seg
    w_nsa = jnp.concatenate([nsa_q, nsa_kv, _pad_cols(nsa_g, 128), nsa_z], axis=1)
    w_lru = jnp.concatenate([lru_x, lru_z], axis=1)
    w_gla = jnp.concatenate([gla_q, gla_k, gla_v, _pad_cols(gla_a, 128), gla_z], axis=1)
    w_mla = jnp.concatenate([mla_z, mla_cq, mla_ckv, _pad_cols(mla_kr, 128)], axis=1)
    w_merge = w_in[:, offs[-1]:]
    return tuple(w.astype(BF16) for w in (w_nsa, w_lru, w_gla, w_mla, w_merge))


def _layer_params(params, layer):
    return {k: v[layer] for k, v in params.items()}


def _prompt_layer(x, p, tabs):
    b, t, d = x.shape
    n = b * t
    w_nsa, w_lru, w_gla, w_mla, w_merge = _input_weights(p["w_in"])
    xn = _rmsnorm_bf16(x.reshape(n, d), p["norm_g"], 512)
    seg = lambda w, name: _matmul(xn, w, 512, name).reshape(b, t, -1)
    h_nsa, h_lru, h_gla, h_mla = (seg(w_nsa, "in_nsa"), seg(w_lru, "in_lru"), seg(w_gla, "in_gla"),
                                  seg(w_mla, "in_mla"))
    nsa_tabs, mla_tabs = tabs
    tq = 128
    qn, qr, rows, win, gate = _nsa_prep(h_nsa, nsa_tabs, p["nsa_q_g"], p["nsa_k_g"], tq)
    comp = _compress(rows, p["nsa_cmp_w"])
    o_cmp, sel = _nsa_cmp(qn, comp, tq)
    br_nsa = _nsa_attn(qr, sel, rows, win, o_cmp, gate, h_nsa, tq)
    br_lru, h_last, conv_new = _lru_prompt(h_lru, p, 256)
    br_gla, gla_state = _gla_prompt(h_gla, p)
    qcat, lat = _mla_prep(h_mla, mla_tabs, p, tq)
    br_mla = _mla_attn(qcat, lat, p["mla_w_uv"], h_mla, 256)
    flat = lambda a: a.reshape(n, BRANCH_W)
    merged = _merge(xn, [flat(br_nsa), flat(br_lru), flat(br_gla), flat(br_mla)], w_merge,
                    p["w_branch"].astype(BF16), MERGE_ROWS, MERGE_COLS)
    y = _outproj(merged, p["w_out"].astype(BF16), x.reshape(n, d), 256).reshape(b, t, d)
    keep = min(WINDOW, t)
    states = (rows.reshape(b, t, 4, NSA_HD), win[:, t - keep:].reshape(b, keep, 2, NSA_HD),
              h_last.reshape(b, BRANCH_W), conv_new, gla_state, lat)
    return y, states


def _sample_layer(x, p, layer, caches, tabs):
    s, t, d = x.shape
    cache_nsa_kv, cache_nsa_win, state_lru_h, state_lru_conv, state_gla, cache_mla, page_table = caches
    w_nsa, w_lru, w_gla, w_mla, w_merge = _input_weights(p["w_in"])
    x2 = x.reshape(s, d)
    xn = _rmsnorm_bf16(x2, p["norm_g"], s)
    h_nsa = _matmul(xn, w_nsa, s, "in_nsa_s")
    h_lru = _matmul(xn, w_lru, s, "in_lru_s")
    h_gla = _matmul(xn, w_gla, s, "in_gla_s")
    h_mla = _matmul(xn, w_mla, s, "in_mla_s")
    nsa_tabs, mla_tabs = tabs
    qn, qr, rows, win, gate = _nsa_prep(h_nsa[None], nsa_tabs, p["nsa_q_g"], p["nsa_k_g"], s)
    heads = lambda a: a.reshape(s, NSA_HEADS, NSA_HD)
    n_pages = page_table.shape[1]
    o_cmp, sel = _nsa_cmp_sample(cache_nsa_kv, layer, page_table, heads(qn[0]), p["nsa_cmp_w"],
                                 min(CMP_STREAM_PAGES, n_pages))
    kvr = jnp.concatenate([rows[0], win[0]], axis=1).reshape(s, 1, 384)
    o_slc, o_win, new_win_t = _nsa_slc_sample(cache_nsa_kv, cache_nsa_win, layer, page_table,
                                              sel.reshape(s, 128), heads(qr[0]), kvr)
    flat = lambda a: a.reshape(s, BRANCH_W)
    br_nsa = _nsa_post_sample(flat(o_cmp), flat(o_slc), flat(o_win), gate[0], h_nsa[:, 1024:1536])
    br_lru, h_new = _lru_sample(h_lru, state_lru_conv[layer], state_lru_h[layer], p)
    conv_new = jnp.swapaxes(jnp.stack([state_lru_conv[layer, 1], state_lru_conv[layer, 2], h_lru[:, 0:512]]), 0, 1)
    br_gla, gla_state = _gla_sample(h_gla, state_gla, layer, p)
    qcat_t, lat = _mla_prep(h_mla[None], mla_tabs, p, s)
    o_lat = _mla_sample_attn(cache_mla, layer, page_table, jnp.transpose(qcat_t[0], (2, 0, 1)),
                             lat[0].reshape(s, 1, MLA_ROW), min(MLA_STREAM_PAGES, n_pages))
    br_mla = _mla_post_sample(o_lat, p["mla_w_uv"], h_mla[:, 0:512])
    merged = _merge(xn, [br_nsa, br_lru, br_gla, br_mla], w_merge, p["w_branch"].astype(BF16), s, MERGE_COLS)
    y = _outproj(merged, p["w_out"].astype(BF16), x2, s).reshape(s, t, d)
    states = (rows[0].reshape(s, 1, 4, NSA_HD), jnp.transpose(new_win_t[0], (0, 3, 1, 2)), h_new, conv_new,
              gla_state, lat[0].reshape(s, 1, MLA_ROW))
    return y, states


def kernel(x_prompt, x_sample, cache_nsa_kv, cache_nsa_win, state_lru_h, state_lru_conv, state_gla, cache_mla,
           page_table, norm_g, w_in, nsa_q_g, nsa_k_g, nsa_cmp_w, lru_conv_w, lru_conv_b, lru_wa, lru_ba, lru_wx,
           lru_bx, lru_lam, gla_w_a2, gla_b_a, gla_out_g, mla_cq_g, mla_w_uq, mla_q_g, mla_ckv_g, mla_kr_g,
           mla_w_uk, mla_w_uv, w_branch, w_out):
    params = dict(norm_g=norm_g, w_in=w_in, nsa_q_g=nsa_q_g, nsa_k_g=nsa_k_g, nsa_cmp_w=nsa_cmp_w,
                  lru_conv_w=lru_conv_w, lru_conv_b=lru_conv_b, lru_wa=lru_wa, lru_ba=lru_ba, lru_wx=lru_wx,
                  lru_bx=lru_bx, lru_lam=lru_lam, gla_w_a2=gla_w_a2, gla_b_a=gla_b_a, gla_out_g=gla_out_g,
                  mla_cq_g=mla_cq_g, mla_w_uq=mla_w_uq, mla_q_g=mla_q_g, mla_ckv_g=mla_ckv_g,
                  mla_kr_g=mla_kr_g, mla_w_uk=mla_w_uk, mla_w_uv=mla_w_uv, w_branch=w_branch, w_out=w_out)
    depth = w_in.shape[0]
    assert x_prompt.shape[2] == N_BRANCH * BRANCH_W and x_sample.shape[1] == 1
    t = x_prompt.shape[1]
    n_seq, n_pages = page_table.shape
    past_len = n_pages * PAGE_SIZE
    assert t % 256 == 0 and past_len % SLC_BLOCK == 0 and cache_nsa_win.shape[2] == WINDOW
    caches = (jnp.transpose(cache_nsa_kv, (0, 1, 3, 4, 2)), jnp.transpose(cache_nsa_win, (0, 1, 3, 4, 2)),
              state_lru_h, jnp.transpose(state_lru_conv, (0, 2, 1, 3)), state_gla,
              jnp.transpose(cache_mla, (0, 1, 3, 2)), page_table)
    tabs_p = _rope_tables(jnp.arange(t))
    tabs_s = _rope_tables(jnp.full((n_seq,), past_len))
    y_p, y_s = x_prompt, x_sample
    sp, ss = [], []
    for layer in range(depth):
        p = _layer_params(params, layer)
        y_p, st = _prompt_layer(y_p, p, tabs_p)
        sp.append(st)
        y_s, st = _sample_layer(y_s, p, layer, caches, tabs_s)
        ss.append(st)
    stack = lambda lst, i: jnp.stack([s[i] for s in lst])
    return (y_p, y_s, stack(sp, 0), stack(ss, 0), stack(sp, 1), stack(ss, 1), stack(sp, 2), stack(ss, 2),
            stack(sp, 3), stack(ss, 3), stack(sp, 4), stack(ss, 4), stack(sp, 5), stack(ss, 5))
```

```python
import functools

import numpy as np
import jax
import jax.numpy as jnp
from jax import lax
from jax.experimental import pallas as pl
from jax.experimental.pallas import tpu as pltpu

F32 = jnp.float32
BF16 = jnp.bfloat16
NEG = -1e30

N_BRANCH = 4
NSA_HEADS = 8
NSA_HD = 64
CMP_BLOCK = 32
CMP_STRIDE = 16
SLC_BLOCK = 64
SLC_TOPN = 16
WINDOW = 512
PAGE_SIZE = 128
LRU_BLOCKS = 8
CONV_W = 4
LRU_C = 8.0
GLA_HEADS = 4
GLA_DK = 64
GLA_DV = 128
GLA_RANK = 16
GLA_TAU = 16.0
GLA_CHUNK = 128
MLA_HEADS = 4
MLA_NOPE = 128
MLA_ROPE = 64
MLA_VHD = 128
MLA_Q_LORA = 384
MLA_KV_LORA = 128
MLA_ROW = MLA_KV_LORA + MLA_ROPE
ROPE_THETA = 10000.0
NORM_EPS = 1e-6

BRANCH_W = 512
NSA_SCALE = NSA_HD ** -0.5
MLA_SCALE = (MLA_NOPE + MLA_ROPE) ** -0.5

SEG_NSA = 1536
SEG_LRU = 1024
SEG_GLA = 1664
SEG_MLA = 1152

VMEM_LIMIT = 48 * 1024 * 1024
CMP_STREAM_PAGES = 16
MLA_STREAM_PAGES = 32
CMP_STREAM_SLOTS = 10
MLA_STREAM_SLOTS = 6
GLA_SAMPLE_SEQS = 4
MERGE_ROWS = 1024
MERGE_COLS = 1024


def _cparams(*sem):
    return pltpu.CompilerParams(dimension_semantics=sem, vmem_limit_bytes=VMEM_LIMIT)


def _dot(a, b):
    return jnp.dot(a.astype(BF16), b.astype(BF16), preferred_element_type=F32)


def _dot_nt(a, b):
    return lax.dot_general(a.astype(BF16), b.astype(BF16), (((1,), (1,)), ((), ())),
                           preferred_element_type=F32)


def _dot_tn(a, b):
    return lax.dot_general(a.astype(BF16), b.astype(BF16), (((0,), (0,)), ((), ())),
                           preferred_element_type=F32)


def _split3(a):
    a0 = a.astype(BF16)
    r = a - a0.astype(F32)
    a1 = r.astype(BF16)
    a2 = (r - a1.astype(F32)).astype(BF16)
    return a0, a1, a2


def _dot_x3(a, b_exact):
    a0, a1, a2 = _split3(a)
    d = functools.partial(jnp.dot, preferred_element_type=F32)
    return d(a0, b_exact) + d(a1, b_exact) + d(a2, b_exact)


def _x3_dot(a_exact, b):
    b0, b1, b2 = _split3(b)
    d = functools.partial(jnp.dot, preferred_element_type=F32)
    return d(a_exact, b0) + d(a_exact, b1) + d(a_exact, b2)


def _rope64(x, cos, sin_signed):
    w = x.shape[1]
    lane = lax.broadcasted_iota(jnp.int32, x.shape, 1)
    swapped = jnp.where((lane & 63) < 32, pltpu.roll(x, w - 32, 1), pltpu.roll(x, 32, 1))
    return x * cos + swapped * sin_signed


def _heads_to_rows(x, n_heads, hd):
    return jnp.concatenate([x[:, hd * h:hd * (h + 1)] for h in range(n_heads)], axis=0)


def _rows_to_heads(x, n_heads):
    t = x.shape[0] // n_heads
    return jnp.concatenate([x[t * h:t * (h + 1)] for h in range(n_heads)], axis=1)


def _silu(z):
    return z * jax.nn.sigmoid(z)


def _norm_kernel(x_ref, g_ref, o_ref):
    x = x_ref[...]
    ms = jnp.mean(x * x, axis=-1, keepdims=True)
    o_ref[...] = (x * lax.rsqrt(ms + NORM_EPS) * g_ref[...]).astype(o_ref.dtype)


def _rmsnorm_bf16(x, g, tm):
    n, d = x.shape
    return pl.pallas_call(
        _norm_kernel, grid=(n // tm,),
        in_specs=[pl.BlockSpec((tm, d), lambda i: (i, 0)), pl.BlockSpec((1, d), lambda i: (0, 0))],
        out_specs=pl.BlockSpec((tm, d), lambda i: (i, 0)),
        out_shape=jax.ShapeDtypeStruct((n, d), BF16),
        compiler_params=_cparams("parallel"), name="rmsnorm")(x, g.reshape(1, d))


def _mm_kernel(x_ref, w_ref, o_ref):
    o_ref[...] = jnp.dot(x_ref[...], w_ref[...], preferred_element_type=F32)


def _matmul(x, w, tm, name):
    n, k = x.shape
    c = w.shape[1]
    return pl.pallas_call(
        _mm_kernel, grid=(n // tm,),
        in_specs=[pl.BlockSpec((tm, k), lambda i: (i, 0)), pl.BlockSpec((k, c), lambda i: (0, 0))],
        out_specs=pl.BlockSpec((tm, c), lambda i: (i, 0)),
        out_shape=jax.ShapeDtypeStruct((n, c), F32),
        compiler_params=_cparams("parallel"), name=name)(x, w)


def _merge_kernel(xn_ref, b0_ref, b1_ref, b2_ref, b3_ref, wm_ref, wb_ref, o_ref, acc_ref):
    n = pl.program_id(2)

    @pl.when(n == 0)
    def _():
        acc_ref[...] = jnp.zeros_like(acc_ref)

    gate = jax.nn.sigmoid(jnp.dot(xn_ref[...], wm_ref[...], preferred_element_type=F32))
    for k, b_ref in enumerate((b0_ref, b1_ref, b2_ref, b3_ref)):
        @pl.when(n == k)
        def _(b_ref=b_ref):
            proj = jnp.dot(b_ref[...], wb_ref[0], preferred_element_type=F32)
            acc_ref[...] += gate * proj

    @pl.when(n == N_BRANCH - 1)
    def _():
        o_ref[...] = acc_ref[...].astype(o_ref.dtype)


def _merge(xn, branches, w_merge, w_branch, tm, tn):
    n, d = xn.shape
    bw = branches[0].shape[1]
    cols = d // tn
    bspec = pl.BlockSpec((tm, bw), lambda i, c, j: (i, 0))
    return pl.pallas_call(
        _merge_kernel, grid=(n // tm, cols, N_BRANCH),
        in_specs=[pl.BlockSpec((tm, d), lambda i, c, j: (i, 0)), bspec, bspec, bspec, bspec,
                  pl.BlockSpec((d, tn), lambda i, c, j: (0, j * cols + c)),
                  pl.BlockSpec((1, bw, tn), lambda i, c, j: (j, 0, c))],
        out_specs=pl.BlockSpec((tm, tn), lambda i, c, j: (i, c)),
        out_shape=jax.ShapeDtypeStruct((n, d), BF16),
        scratch_shapes=[pltpu.VMEM((tm, tn), F32)],
        compiler_params=_cparams("parallel", "parallel", "arbitrary"), name="merge",
    )(xn, *branches, w_merge, w_branch)


def _outproj_kernel(m_ref, w_ref, x_ref, o_ref):
    o_ref[...] = x_ref[...] + jnp.dot(m_ref[...], w_ref[...], preferred_element_type=F32)


def _outproj(merged, w_out, x, tm):
    n, d = x.shape
    return pl.pallas_call(
        _outproj_kernel, grid=(n // tm,),
        in_specs=[pl.BlockSpec((tm, d), lambda i: (i, 0)), pl.BlockSpec((d, d), lambda i: (0, 0)),
                  pl.BlockSpec((tm, d), lambda i: (i, 0))],
        out_specs=pl.BlockSpec((tm, d), lambda i: (i, 0)),
        out_shape=jax.ShapeDtypeStruct((n, d), F32),
        compiler_params=_cparams("parallel"), name="outproj")(merged, w_out, x)


def _nsa_prep_kernel(h_ref, cq_ref, sq_ref, ckv_ref, skv_ref, qg_ref, kvg_ref, kvm_ref, bdq_ref, bdkv_ref,
                     qn_ref, qr_ref, rows_ref, win_ref, gate_ref):
    h = h_ref[0]
    q = h[:, 0:512]
    kv = h[:, 512:896]
    ms_q = _dot_x3(q * q, bdq_ref[...]) * (1.0 / NSA_HD)
    qn = q * lax.rsqrt(ms_q + NORM_EPS) * qg_ref[...]
    qn_ref[0] = qn
    qr_ref[0] = _rope64(qn, cq_ref[...], sq_ref[...])
    ms_kv = _dot_x3(kv * kv, bdkv_ref[...]) * (1.0 / NSA_HD)
    kvn = kv * lax.rsqrt(ms_kv + NORM_EPS) * kvg_ref[...]
    kvn = jnp.where(kvm_ref[...] > 0.5, kvn, kv)
    kvr = _rope64(kvn, ckv_ref[...], skv_ref[...])
    rows_ref[0] = kvr[:, 0:256]
    win_ref[0] = kvr[:, 256:384]
    gate_ref[0] = jax.nn.sigmoid(h[:, 896:1024])


def _block_diag_ones(width, group):
    idx = np.arange(width) // group
    return jnp.asarray((idx[:, None] == idx[None, :]).astype(np.float32), dtype=BF16)


def _nsa_prep(h, tabs, q_gain, k_gain, tq):
    b, t, _ = h.shape
    cq, sq, ckv, skv = tabs
    one = jnp.ones((NSA_HD,), F32)
    kvg = jnp.concatenate([k_gain[0], one, k_gain[1], one, k_gain[2], one]).reshape(1, 384)
    kvm = jnp.concatenate([one, 0 * one, one, 0 * one, one, 0 * one]).reshape(1, 384)
    qg = jnp.tile(q_gain, NSA_HEADS).reshape(1, 512)
    tile = lambda w: pl.BlockSpec((1, tq, w), lambda bi, i: (bi, i, 0))
    tab = lambda w: pl.BlockSpec((tq, w), lambda bi, i: (i, 0))
    const = lambda r, w: pl.BlockSpec((r, w), lambda bi, i: (0, 0))
    return pl.pallas_call(
        _nsa_prep_kernel, grid=(b, t // tq),
        in_specs=[tile(SEG_NSA), tab(512), tab(512), tab(384), tab(384), const(1, 512), const(1, 384),
                  const(1, 384), const(512, 512), const(384, 384)],
        out_specs=[tile(512), tile(512), tile(256), tile(128), tile(128)],
        out_shape=[jax.ShapeDtypeStruct((b, t, 512), F32), jax.ShapeDtypeStruct((b, t, 512), F32),
                   jax.ShapeDtypeStruct((b, t, 256), F32), jax.ShapeDtypeStruct((b, t, 128), F32),
                   jax.ShapeDtypeStruct((b, t, 128), F32)],
        compiler_params=_cparams("parallel", "parallel"), name="nsa_prep",
    )(h, cq, sq, ckv, skv, qg, kvg, kvm, _block_diag_ones(512, 64), _block_diag_ones(384, 64))


def _rope_base(pos):
    inv = ROPE_THETA ** (-jnp.arange(0, 64, 2, dtype=F32) / 64)
    ang = pos.astype(F32)[:, None] * inv[None, :]
    c = jnp.cos(ang)
    s = jnp.sin(ang)
    return jnp.concatenate([c, c], axis=-1), jnp.concatenate([-s, s], axis=-1)


def _rope_tables(pos):
    c, s = _rope_base(pos)
    one = jnp.ones_like(c)
    zero = jnp.zeros_like(s)
    nsa = (jnp.tile(c, (1, NSA_HEADS)), jnp.tile(s, (1, NSA_HEADS)),
           jnp.concatenate([one, one, c, one, c, one], axis=1),
           jnp.concatenate([zero, zero, s, zero, s, zero], axis=1))
    mla = (jnp.tile(c, (1, MLA_HEADS)), jnp.tile(s, (1, MLA_HEADS)),
           jnp.concatenate([c, one], axis=1), jnp.concatenate([s, zero], axis=1))
    return nsa, mla


def _compress_kernel(rows_ref, w1_ref, w2_ref, comp_ref):
    t = rows_ref.shape[1]
    n = t // CMP_STRIDE
    first = jnp.zeros((n, 128), F32)
    second = jnp.zeros((n, 128), F32)
    for j in range(CMP_STRIDE):
        r = rows_ref[0, pl.ds(j, n, stride=CMP_STRIDE), :]
        first = first + r * w1_ref[j:j + 1, :]
        second = second + r * w2_ref[j:j + 1, :]
    nxt = pltpu.roll(second, n - 1, 0)
    row = lax.broadcasted_iota(jnp.int32, (n, 128), 0)
    comp_ref[0] = jnp.where(row < n - 1, first + nxt, 0.0)


def _cmp_weights(w_pos):
    w1 = jnp.concatenate([w_pos[0, :CMP_STRIDE], w_pos[1, :CMP_STRIDE]], axis=1)
    w2 = jnp.concatenate([w_pos[0, CMP_STRIDE:], w_pos[1, CMP_STRIDE:]], axis=1)
    return w1, w2


def _compress(rows, w_pos):
    b, t, _ = rows.shape
    w1, w2 = _cmp_weights(w_pos)
    n = t // CMP_STRIDE
    return pl.pallas_call(
        _compress_kernel, grid=(b,),
        in_specs=[pl.BlockSpec((1, t, 128), lambda bi: (bi, 0, 0)),
                  pl.BlockSpec((CMP_STRIDE, 128), lambda bi: (0, 0)),
                  pl.BlockSpec((CMP_STRIDE, 128), lambda bi: (0, 0))],
        out_specs=pl.BlockSpec((1, n, 128), lambda bi: (bi, 0, 0)),
        out_shape=jax.ShapeDtypeStruct((b, n, 128), F32),
        compiler_params=_cparams("parallel"), name="nsa_compress")(rows, w1, w2)


def _score_matrix(n_cmp, n_slc):
    m = np.zeros((n_cmp, n_slc), np.float32)
    per = SLC_BLOCK // CMP_STRIDE
    for n in range(n_cmp):
        for s in (n, n + 1):
            if s // per < n_slc:
                m[n, s // per] += 1.0
    return jnp.asarray(m, dtype=BF16)


def _nsa_cmp_kernel(qn_ref, comp_ref, m_ref, o_ref, sel_ref, *, tq):
    i = pl.program_id(1)
    qh = _heads_to_rows(qn_ref[0], NSA_HEADS, NSA_HD)
    comp = comp_ref[0]
    nc = comp.shape[0]
    s = _dot_nt(qh, comp[:, 0:64]) * NSA_SCALE
    qpos = i * tq + lax.broadcasted_iota(jnp.int32, (tq, nc), 0)
    blk_end = lax.broadcasted_iota(jnp.int32, (tq, nc), 1) * CMP_STRIDE + (CMP_BLOCK - 1)
    mask = (blk_end <= qpos)[None]
    s3 = jnp.where(mask, s.reshape(NSA_HEADS, tq, nc), NEG)
    mx = jnp.max(s3, axis=-1, keepdims=True)
    e = jnp.where(mask, jnp.exp(s3 - mx), 0.0)
    p3 = e / jnp.maximum(jnp.sum(e, axis=-1, keepdims=True), 1e-30)
    o = _dot(p3.reshape(NSA_HEADS * tq, nc), comp[:, 64:128])
    o_ref[0] = _rows_to_heads(o, NSA_HEADS)
    score_t = _dot_x3(jnp.sum(p3, axis=0), m_ref[...]).T
    nsb = score_t.shape[0]
    blk = lax.broadcasted_iota(jnp.int32, (nsb, tq), 0)
    qp = i * tq + lax.broadcasted_iota(jnp.int32, (nsb, tq), 1)
    cur = jnp.right_shift(qp, 6)
    valid = blk * SLC_BLOCK <= qp
    forced = (blk == 0) | (blk == cur) | (blk == cur - 1)
    sc = jnp.where(valid, jnp.where(forced, 1e30, score_t), -1e30)
    rank = jnp.zeros((nsb, tq), F32)
    for b in range(nsb):
        other = sc[b:b + 1, :]
        rank = rank + jnp.where((other > sc) | ((other == sc) & (blk > b)), 1.0, 0.0)
    sel_ref[0] = jnp.where((rank < SLC_TOPN) & valid, 1.0, 0.0)


def _nsa_cmp(qn, comp, tq):
    b, t, _ = qn.shape
    nc = comp.shape[1]
    nsb = t // SLC_BLOCK
    return pl.pallas_call(
        functools.partial(_nsa_cmp_kernel, tq=tq), grid=(b, t // tq),
        in_specs=[pl.BlockSpec((1, tq, 512), lambda bi, i: (bi, i, 0)),
                  pl.BlockSpec((1, nc, 128), lambda bi, i: (bi, 0, 0)),
                  pl.BlockSpec((nc, nsb), lambda bi, i: (0, 0))],
        out_specs=[pl.BlockSpec((1, tq, 512), lambda bi, i: (bi, i, 0)),
                   pl.BlockSpec((1, nsb, tq), lambda bi, i: (bi, 0, i))],
        out_shape=[jax.ShapeDtypeStruct((b, t, 512), F32), jax.ShapeDtypeStruct((b, nsb, t), F32)],
        compiler_params=_cparams("parallel", "parallel"), name="nsa_cmp")(qn, comp, _score_matrix(nc, nsb))


def _gate_expand(c):
    g = np.zeros((128, 512), np.float32)
    for h in range(NSA_HEADS):
        g[3 * h + c, NSA_HD * h:NSA_HD * (h + 1)] = 1.0
    return g


def _gate_mats():
    return jnp.asarray(np.stack([_gate_expand(c) for c in range(3)]), dtype=BF16)


def _nsa_combine(o_cmp, o_slc, o_win, gate, gx_ref, z):
    g0 = _dot_x3(gate, gx_ref[0])
    g1 = _dot_x3(gate, gx_ref[1])
    g2 = _dot_x3(gate, gx_ref[2])
    return ((g0 * o_cmp + g1 * o_slc + g2 * o_win) * _silu(z)).astype(BF16)


def _flash_reset(m_s, l_s, acc_s):
    m_s[...] = jnp.full_like(m_s, NEG)
    l_s[...] = jnp.zeros_like(l_s)
    acc_s[...] = jnp.zeros_like(acc_s)


def _flash_update(s_t, v, m_s, l_s, acc_s):
    m_new = jnp.maximum(m_s[...], jnp.max(s_t, axis=0, keepdims=True))
    alpha = jnp.exp(m_s[...] - m_new)
    p = jnp.exp(s_t - m_new)
    l_s[...] = alpha * l_s[...] + jnp.sum(p, axis=0, keepdims=True)
    acc_s[...] = alpha * acc_s[...] + _dot_tn(v, p)
    m_s[...] = m_new


def _nsa_attn_kernel(qr_ref, sel_ref, rows_ref, win_ref, ocmp_ref, gate_ref, gx_ref, z_ref, out_ref,
                     m_s, l_s, acc_s, *, tq, tk):
    i = pl.program_id(1)
    q = qr_ref[0] * NSA_SCALE
    q_t = jnp.concatenate([q[:, NSA_HD * h:NSA_HD * (h + 1)].T for h in range(NSA_HEADS)], axis=1).astype(BF16)
    sel_t = sel_ref[0].astype(BF16)
    nsb = sel_t.shape[0]
    koff = lax.broadcasted_iota(jnp.int32, (tq, tq), 0)
    qpos = i * tq + lax.broadcasted_iota(jnp.int32, (tq, tq), 1)
    stats = (m_s, l_s, acc_s)

    def scores(k, mask):
        bias = jnp.where(mask, 0.0, NEG)
        return _dot(k, q_t) + jnp.concatenate([bias] * NSA_HEADS, axis=1)

    def finish():
        o_t = acc_s[...] / l_s[...]
        return jnp.concatenate([o_t[:, tq * h:tq * (h + 1)].T for h in range(NSA_HEADS)], axis=1)

    def slc_tile(j, carry):
        start = pl.multiple_of(j * tk, tk)
        k = rows_ref[0, pl.ds(start, tk), 128:192]
        v = rows_ref[0, pl.ds(start, tk), 192:256]
        kpos = start + lax.broadcasted_iota(jnp.int32, (tk, tq), 0)
        expand = (lax.broadcasted_iota(jnp.int32, (tk, nsb), 1)
                  == jnp.right_shift(start + lax.broadcasted_iota(jnp.int32, (tk, nsb), 0), 6))
        selk = jnp.dot(jnp.where(expand, 1.0, 0.0).astype(BF16), sel_t, preferred_element_type=F32)
        visible = kpos <= i * tq + lax.broadcasted_iota(jnp.int32, (tk, tq), 1)
        _flash_update(scores(k, visible & (selk > 0.5)), v, *stats)
        return carry

    _flash_reset(*stats)
    lax.fori_loop(0, (i * tq + tq + tk - 1) // tk, slc_tile, 0)
    o_slc = finish()

    def win_tile(n, carry):
        start = pl.multiple_of((i - n) * tq, tq)
        k = win_ref[0, pl.ds(start, tq), 0:64]
        v = win_ref[0, pl.ds(start, tq), 64:128]
        kpos = start + koff
        _flash_update(scores(k, (kpos <= qpos) & (kpos > qpos - WINDOW)), v, *stats)
        return carry

    _flash_reset(*stats)
    lax.fori_loop(0, jnp.minimum(i, WINDOW // tq) + 1, win_tile, 0)
    o_win = finish()
    out_ref[0] = _nsa_combine(ocmp_ref[0], o_slc, o_win, gate_ref[0], gx_ref, z_ref[0])


def _nsa_attn(qr, sel_t, rows, win, o_cmp, gate, h, tq):
    b, t, _ = qr.shape
    nsb = sel_t.shape[1]
    tile = lambda w: pl.BlockSpec((1, tq, w), lambda bi, i: (bi, i, 0))
    return pl.pallas_call(
        functools.partial(_nsa_attn_kernel, tq=tq, tk=4 * tq), grid=(b, t // tq),
        in_specs=[tile(512), pl.BlockSpec((1, nsb, tq), lambda bi, i: (bi, 0, i)),
                  pl.BlockSpec((1, t, 256), lambda bi, i: (bi, 0, 0)),
                  pl.BlockSpec((1, t, 128), lambda bi, i: (bi, 0, 0)),
                  tile(512), tile(128),
                  pl.BlockSpec((3, 128, 512), lambda bi, i: (0, 0, 0)),
                  pl.BlockSpec((1, tq, 512), lambda bi, i: (bi, i, 2))],
        out_specs=tile(512),
        out_shape=jax.ShapeDtypeStruct((b, t, 512), BF16),
        scratch_shapes=[pltpu.VMEM((1, NSA_HEADS * tq), F32), pltpu.VMEM((1, NSA_HEADS * tq), F32),
                        pltpu.VMEM((NSA_HD, NSA_HEADS * tq), F32)],
        compiler_params=_cparams("parallel", "arbitrary"), name="nsa_attn",
    )(qr, sel_t, rows, win, o_cmp, gate, _gate_mats(), h)


def _page_stream(pt_ref, buf, sem, n_groups, group, page_src, compute):
    slots = buf.shape[0]
    total = pt_ref.shape[0] * n_groups

    def copies(it):
        b = it // n_groups
        g = it - b * n_groups
        slot = it % slots
        return [pltpu.make_async_copy(page_src(pt_ref[b, g * group + k]), buf.at[slot, k], sem.at[slot])
                for k in range(group)]

    for it0 in range(min(slots - 1, total)):
        for cp in copies(it0):
            cp.start()

    def body(it, carry):
        for cp in copies(it):
            cp.wait()

        @pl.when(it + (slots - 1) < total)
        def _():
            for cp in copies(it + (slots - 1)):
                cp.start()

        b = it // n_groups
        compute(b, it - b * n_groups, it % slots)
        return carry

    lax.fori_loop(0, total, body, 0)


def _nsa_cmp_sample_kernel(pt_ref, cache_ref, qn_ref, wt_ref, g8_ref, m_ref, o_ref, sel_ref,
                           buf, sem, sums_s, *, layer, n_pages, past_len):
    per_page = PAGE_SIZE // CMP_STRIDE
    group = buf.shape[1]
    n_groups = n_pages // group
    nc = n_pages * per_page
    sums_s[pl.ds(nc, 8), :] = jnp.zeros((8, 4 * NSA_HD), F32)

    def compute(b, g, slot):
        base = g * (group * per_page)
        for k in range(group):
            k_t = buf[slot, k, 0]
            v_t = buf[slot, k, 1]
            stacked = jnp.concatenate([(k_t * wt_ref[0]).astype(BF16), (k_t * wt_ref[1]).astype(BF16),
                                       (v_t * wt_ref[2]).astype(BF16), (v_t * wt_ref[3]).astype(BF16)], axis=0)
            off = pl.multiple_of(base + k * per_page, per_page)
            sums_s[pl.ds(off, per_page), :] = lax.dot_general(
                g8_ref[...], stacked, (((1,), (1,)), ((), ())), preferred_element_type=F32)

        @pl.when(g == n_groups - 1)
        def _():
            _nsa_cmp_sample_finish(b, qn_ref, m_ref, o_ref, sel_ref, sums_s, nc=nc, past_len=past_len)

    _page_stream(pt_ref, buf, sem, n_groups, group, lambda page: cache_ref.at[layer, page, pl.ds(0, 2)], compute)


def _nsa_cmp_sample_finish(b, qn_ref, m_ref, o_ref, sel_ref, sums_s, *, nc, past_len):
    first = sums_s[pl.ds(0, nc), :]
    second = sums_s[pl.ds(1, nc), :]
    ck = first[:, 0:64] + second[:, 64:128]
    cv = first[:, 128:192] + second[:, 192:256]
    q = qn_ref[b]
    s = _dot_nt(q, ck) * NSA_SCALE
    blk_end = lax.broadcasted_iota(jnp.int32, (NSA_HEADS, nc), 1) * CMP_STRIDE + (CMP_BLOCK - 1)
    mask = blk_end <= past_len
    s = jnp.where(mask, s, NEG)
    e = jnp.where(mask, jnp.exp(s - jnp.max(s, axis=-1, keepdims=True)), 0.0)
    p = e / jnp.maximum(jnp.sum(e, axis=-1, keepdims=True), 1e-30)
    o_ref[b] = _dot(p, cv)
    imp = jnp.sum(p, axis=0, keepdims=True)
    score = _dot_x3(imp, m_ref[...])
    nsb = score.shape[1]
    lane = lax.broadcasted_iota(jnp.int32, (nsb, nsb), 1)
    sub = lax.broadcasted_iota(jnp.int32, (nsb, nsb), 0)
    forced_row = (lane == 0) | (lane == nsb - 1)
    sc_row = jnp.where(forced_row, 1e30, jnp.broadcast_to(score, (nsb, nsb)))
    eye = lane == sub
    sc_col = jnp.sum(jnp.where(eye, sc_row, 0.0), axis=1, keepdims=True)
    beats_row = (sc_col > sc_row) | ((sc_col == sc_row) & (sub < lane))
    rank_row = jnp.sum(jnp.where(beats_row, 1.0, 0.0), axis=0, keepdims=True)
    sel_row = rank_row < (SLC_TOPN - 1)
    beats_col = (sc_row > sc_col) | ((sc_row == sc_col) & (lane < sub))
    rank_col = jnp.sum(jnp.where(beats_col, 1.0, 0.0), axis=1, keepdims=True)
    sel_col = rank_col < (SLC_TOPN - 1)
    pos_col = jnp.sum(jnp.where(sel_row & (lane < sub), 1.0, 0.0), axis=1, keepdims=True)
    slot = lax.broadcasted_iota(jnp.int32, (nsb, 128), 1).astype(F32)
    blk_id = lax.broadcasted_iota(jnp.int32, (nsb, 128), 0).astype(F32)
    onehot = sel_col & (pos_col == slot)
    sel_ref[b] = jnp.sum(jnp.where(onehot, blk_id, 0.0), axis=0, keepdims=True).astype(jnp.int32)


def _nsa_cmp_sample(cache_kv_t, layer, page_table, qn, w_pos, pages_per_step):
    s, n_pages = page_table.shape
    past_len = n_pages * PAGE_SIZE
    per_page = PAGE_SIZE // CMP_STRIDE
    nc = n_pages * per_page
    nsb = past_len // SLC_BLOCK
    group = pages_per_step
    assert n_pages % group == 0
    tile_t = lambda w: jnp.tile(w.T, (1, per_page))
    wt = jnp.stack([tile_t(w_pos[0, :CMP_STRIDE]), tile_t(w_pos[0, CMP_STRIDE:]),
                    tile_t(w_pos[1, :CMP_STRIDE]), tile_t(w_pos[1, CMP_STRIDE:])])
    g8 = jnp.asarray((np.arange(per_page)[:, None] == (np.arange(PAGE_SIZE) // CMP_STRIDE)[None, :])
                     .astype(np.float32), dtype=BF16)
    vmem = pl.BlockSpec(memory_space=pltpu.VMEM)
    return pl.pallas_call(
        functools.partial(_nsa_cmp_sample_kernel, layer=layer, n_pages=n_pages, past_len=past_len),
        in_specs=[pl.BlockSpec(memory_space=pltpu.SMEM), pl.BlockSpec(memory_space=pl.ANY),
                  vmem, vmem, vmem, vmem],
        out_specs=[vmem, vmem],
        out_shape=[jax.ShapeDtypeStruct((s, NSA_HEADS, NSA_HD), F32),
                   jax.ShapeDtypeStruct((s, 1, 128), jnp.int32)],
        scratch_shapes=[pltpu.VMEM((CMP_STREAM_SLOTS, group, 2, NSA_HD, PAGE_SIZE), F32),
                        pltpu.SemaphoreType.DMA((CMP_STREAM_SLOTS,)),
                        pltpu.VMEM((nc + 8, 4 * NSA_HD), F32)],
        compiler_params=pltpu.CompilerParams(vmem_limit_bytes=VMEM_LIMIT), name="nsa_cmp_sample",
    )(page_table, cache_kv_t, qn, wt, g8, _score_matrix(nc, nsb))


def _attend_with_new(q, s_past, v_t_parts, k_new, v_new):
    s_new = jnp.sum(q * k_new, axis=-1, keepdims=True)
    m = jnp.maximum(jnp.max(s_past, axis=-1, keepdims=True), s_new)
    p = jnp.exp(s_past - m)
    p_new = jnp.exp(s_new - m)
    acc = p_new * v_new
    off = 0
    for v_t in v_t_parts:
        n = v_t.shape[1]
        acc = acc + _dot_nt(p[:, off:off + n], v_t)
        off += n
    return acc / (jnp.sum(p, axis=-1, keepdims=True) + p_new)


def _nsa_slc_sample_kernel(sel_ref, pt_ref, *refs, n_sel, past_len):
    b = pl.program_id(0)
    blk_refs = refs[:n_sel]
    qr_ref, new_ref, winbuf_ref, oslc_ref, owin_ref, newwin_ref = refs[n_sel:]
    q = qr_ref[0] * NSA_SCALE
    new = new_ref[0]
    per_page = PAGE_SIZE // SLC_BLOCK
    lane = lax.broadcasted_iota(jnp.int32, (NSA_HEADS, PAGE_SIZE), 1)
    parts = []
    for k, r in enumerate(blk_refs):
        half = sel_ref[b, k] % per_page
        parts.append(jnp.where(jnp.right_shift(lane, 6) == half, _dot(q, r[0, 0, 0]), NEG))
    oslc_ref[0] = _attend_with_new(q, jnp.concatenate(parts, axis=1), [r[0, 0, 1] for r in blk_refs],
                                   new[:, 128:192], new[:, 192:256])
    lw = winbuf_ref.shape[4]
    kw_t = winbuf_ref[0, 0, 0]
    vw_t = winbuf_ref[0, 0, 1]
    k_pos = past_len - lw + lax.broadcasted_iota(jnp.int32, (NSA_HEADS, lw), 1)
    s = jnp.where(k_pos > past_len - WINDOW, _dot(q, kw_t), NEG)
    owin_ref[0] = _attend_with_new(q, s, [vw_t], new[:, 256:320], new[:, 320:384])
    eye = (lax.broadcasted_iota(jnp.int32, (NSA_HD, NSA_HD), 0)
           == lax.broadcasted_iota(jnp.int32, (NSA_HD, NSA_HD), 1))
    to_col = lambda r: jnp.sum(jnp.where(eye, r, 0.0), axis=1, keepdims=True)
    last = lax.broadcasted_iota(jnp.int32, (NSA_HD, lw), 1) == lw - 1
    newwin_ref[0, 0, 0] = jnp.where(last, to_col(new[:, 256:320]), pltpu.roll(kw_t, lw - 1, 1))
    newwin_ref[0, 0, 1] = jnp.where(last, to_col(new[:, 320:384]), pltpu.roll(vw_t, lw - 1, 1))


def _nsa_slc_sample(cache_kv_t, cache_win_t, layer, page_table, sel, qr, kvr):
    s, n_pages = page_table.shape
    past_len = n_pages * PAGE_SIZE
    lw = cache_win_t.shape[4]
    n_sel = SLC_TOPN - 1
    per_page = PAGE_SIZE // SLC_BLOCK

    def blk_spec(k):
        return pl.BlockSpec((1, 1, 2, NSA_HD, PAGE_SIZE),
                            lambda b, sl, pt: (layer, pt[b, sl[b, k] // per_page], 1, 0, 0))

    grid_spec = pltpu.PrefetchScalarGridSpec(
        num_scalar_prefetch=2, grid=(s,),
        in_specs=[blk_spec(k) for k in range(n_sel)]
        + [pl.BlockSpec((1, NSA_HEADS, NSA_HD), lambda b, sl, pt: (b, 0, 0)),
           pl.BlockSpec((1, 1, 384), lambda b, sl, pt: (b, 0, 0)),
           pl.BlockSpec((1, 1, 2, NSA_HD, lw), lambda b, sl, pt: (layer, b, 0, 0, 0))],
        out_specs=[pl.BlockSpec((1, NSA_HEADS, NSA_HD), lambda b, sl, pt: (b, 0, 0)),
                   pl.BlockSpec((1, NSA_HEADS, NSA_HD), lambda b, sl, pt: (b, 0, 0)),
                   pl.BlockSpec((1, 1, 2, NSA_HD, lw), lambda b, sl, pt: (0, b, 0, 0, 0))])
    return pl.pallas_call(
        functools.partial(_nsa_slc_sample_kernel, n_sel=n_sel, past_len=past_len),
        grid_spec=grid_spec,
        out_shape=[jax.ShapeDtypeStruct((s, NSA_HEADS, NSA_HD), F32),
                   jax.ShapeDtypeStruct((s, NSA_HEADS, NSA_HD), F32),
                   jax.ShapeDtypeStruct((1, s, 2, NSA_HD, lw), F32)],
        compiler_params=_cparams("parallel"), name="nsa_slc_sample",
    )(sel, page_table, *([cache_kv_t] * n_sel), qr, kvr, cache_win_t)


def _nsa_post_sample_kernel(ocmp_ref, oslc_ref, owin_ref, gate_ref, gx_ref, z_ref, out_ref):
    out_ref[...] = _nsa_combine(ocmp_ref[...], oslc_ref[...], owin_ref[...], gate_ref[...], gx_ref, z_ref[...])


def _nsa_post_sample(o_cmp, o_slc, o_win, gate, z):
    s = gate.shape[0]
    return pl.pallas_call(
        _nsa_post_sample_kernel, out_shape=jax.ShapeDtypeStruct((s, 512), BF16), name="nsa_post_sample",
    )(o_cmp, o_slc, o_win, gate, _gate_mats(), z)


def _lru_gates(u, wa_ref, ba_ref, wx_ref, bx_ref, lam_ref):
    r =jax.nn.sigmoid(_dot(u, wa_ref[...]) + ba_ref[...])
    ig = jax.nn.sigmoid(_dot(u, wx_ref[...]) + bx_ref[...])
    log_a = -LRU_C * r * jax.nn.softplus(-lam_ref[...])
    a = jnp.exp(log_a)
    b = jnp.sqrt(jnp.tanh(-log_a) * (a * a + 1.0)) * (ig * u)
    return a, b


def _lru_kernel(h_ref, cw_ref, cb_ref, wa_ref, ba_ref, wx_ref, bx_ref, lam_ref,
                br_ref, hlast_ref, conv_ref, xbuf, hc):
    t = pl.program_id(1)
    tt = h_ref.shape[1]

    @pl.when(t == 0)
    def _():
        xbuf[0:8, :] = jnp.zeros((8, BRANCH_W), F32)
        hc[...] = jnp.zeros_like(hc)

    x = h_ref[0, :, 0:512]
    z = h_ref[0, :, 512:1024]
    xbuf[8:8 + tt, :] = x
    u = cb_ref[...] + xbuf[pl.ds(8, tt), :] * cw_ref[3:4, :]
    for j in range(CONV_W - 1):
        u = u + xbuf[pl.ds(5 + j, tt), :] * cw_ref[j:j + 1, :]
    xbuf[0:8, :] = xbuf[tt:tt + 8, :]
    a, b = _lru_gates(u, wa_ref, ba_ref, wx_ref, bx_ref, lam_ref)
    row = lax.broadcasted_iota(jnp.int32, (tt, BRANCH_W), 0)
    d = 1
    while d < tt:
        keep = row >= d
        b = jnp.where(keep, a * pltpu.roll(b, d, 0) + b, b)
        a = jnp.where(keep, a * pltpu.roll(a, d, 0), a)
        d *= 2
    hseq = a * hc[...] + b
    hc[...] = hseq[tt - 1:tt, :]
    br_ref[0] = (hseq * _silu(z)).astype(BF16)

    @pl.when(t == pl.num_programs(1) - 1)
    def _():
        hlast_ref[0] = hseq[tt - 1:tt, :]
        conv_ref[0] = x[tt - (CONV_W - 1):tt, :]


def _block_diag(w):
    n, c, d = w.shape
    eye = jnp.eye(n, dtype=w.dtype)
    return (w[:, :, None, :] * eye[:, None, :, None]).reshape(n * c, n * d)


def _lru_weights(p):
    row = lambda v: v.reshape(1, BRANCH_W)
    return (p["lru_conv_w"], row(p["lru_conv_b"]), _block_diag(p["lru_wa"]).astype(BF16), row(p["lru_ba"]),
            _block_diag(p["lru_wx"]).astype(BF16), row(p["lru_bx"]), row(p["lru_lam"]))


def _lru_prompt(h, p, tt):
    b, t, _ = h.shape
    const = lambda r, w: pl.BlockSpec((r, w), lambda bi, i: (0, 0))
    return pl.pallas_call(
        _lru_kernel, grid=(b, t // tt),
        in_specs=[pl.BlockSpec((1, tt, SEG_LRU), lambda bi, i: (bi, i, 0)),
                  const(CONV_W, 512), const(1, 512), const(512, 512), const(1, 512), const(512, 512),
                  const(1, 512), const(1, 512)],
        out_specs=[pl.BlockSpec((1, tt, 512), lambda bi, i: (bi, i, 0)),
                   pl.BlockSpec((1, 1, 512), lambda bi, i: (bi, 0, 0)),
                   pl.BlockSpec((1, CONV_W - 1, 512), lambda bi, i: (bi, 0, 0))],
        out_shape=[jax.ShapeDtypeStruct((b, t, 512), BF16), jax.ShapeDtypeStruct((b, 1, 512), F32),
                   jax.ShapeDtypeStruct((b, CONV_W - 1, 512), F32)],
        scratch_shapes=[pltpu.VMEM((tt + 8, 512), F32), pltpu.VMEM((1, 512), F32)],
        compiler_params=_cparams("parallel", "arbitrary"), name="lru_prompt")(h, *_lru_weights(p))


def _lru_sample_kernel(h_ref, c0_ref, c1_ref, c2_ref, h0_ref, cw_ref, cb_ref, wa_ref, ba_ref, wx_ref, bx_ref,
                       lam_ref, br_ref, hnew_ref):
    x = h_ref[:, 0:512]
    z = h_ref[:, 512:1024]
    u = cb_ref[...] + x * cw_ref[3:4, :]
    for j, c_ref in enumerate((c0_ref, c1_ref, c2_ref)):
        u = u + c_ref[...] * cw_ref[j:j + 1, :]
    a, b = _lru_gates(u, wa_ref, ba_ref, wx_ref, bx_ref, lam_ref)
    hnew = a * h0_ref[...] + b
    hnew_ref[...] = hnew
    br_ref[...] = (hnew * _silu(z)).astype(BF16)


def _lru_sample(h, conv_prev_t, h0, p):
    s = h.shape[0]
    c0, c1, c2 = (conv_prev_t[j] for j in range(CONV_W - 1))
    return pl.pallas_call(
        _lru_sample_kernel,
        out_shape=[jax.ShapeDtypeStruct((s, 512), BF16), jax.ShapeDtypeStruct((s, 512), F32)],
        name="lru_sample")(h, c0, c1, c2, h0, *_lru_weights(p))


def _gla_constants(c):
    tri = np.tril(np.ones((c, c), np.float32))
    rows = [tri]
    masks = []
    t = np.arange(c)
    b = c // 2
    while b >= 1:
        ref_row = (t // (2 * b)) * 2 * b + b - 1
        rows.append(tri[ref_row])
        same = (t[:, None] // (2 * b)) == (t[None, :] // (2 * b))
        masks.append((same & ((t[:, None] // b) % 2 == 1) & ((t[None, :] // b) % 2 == 0)).astype(np.float32))
        b //= 2
    return (jnp.asarray(np.concatenate(rows, axis=0), dtype=BF16), jnp.asarray(np.stack(masks), dtype=F32))


def _gla_log_alpha(a_in, wa_ref, ba_ref):
    return jax.nn.log_sigmoid(_dot(a_in, wa_ref[...]) + ba_ref[...]) * (1.0 / GLA_TAU)


def _gla_out(o, gout_ref, z):
    outs = []
    for h in range(GLA_HEADS):
        oh = o[h]
        ms = jnp.mean(oh * oh, axis=-1, keepdims=True)
        outs.append(oh * lax.rsqrt(ms + NORM_EPS) * gout_ref[...])
    return (jnp.concatenate(outs, axis=1) * _silu(z)).astype(BF16)


def _gla_kernel(h_ref, wa_ref, ba_ref, gout_ref, cum_ref, lm_ref, br_ref, state_ref, st_s):
    ci = pl.program_id(0)
    c = h_ref.shape[1]
    n_lvl = lm_ref.shape[0]

    @pl.when(ci == 0)
    def _():
        st_s[...] = jnp.zeros_like(st_s)

    eye = (lax.broadcasted_iota(jnp.int32, (c, c), 0) == lax.broadcasted_iota(jnp.int32, (c, c), 1))
    for bi in range(h_ref.shape[0]):
        h = h_ref[bi]
        q = h[:, 0:256] * (GLA_DK ** -0.5)
        k = h[:, 256:512]
        v = h[:, 512:1024]
        g = _gla_log_alpha(h[:, 1024:1152], wa_ref, ba_ref)
        cums = _x3_dot(cum_ref[...], g)
        bc = cums[0:c]
        b_end = bc[c - 1:c, :]
        q_in = q * jnp.exp(bc)
        k_end = k * jnp.exp(b_end - bc)
        aq = []
        ak = []
        for lv in range(n_lvl):
            ref = cums[(lv + 1) * c:(lv + 2) * c]
            aq.append(q * jnp.exp(jnp.minimum(bc - ref, 0.0)))
            ak.append(k * jnp.exp(jnp.minimum(ref - bc, 0.0)))
        decay_end = jnp.exp(b_end)
        outs = []
        for hd in range(GLA_HEADS):
            ks = slice(GLA_DK * hd, GLA_DK * (hd + 1))
            vh = v[:, GLA_DV * hd:GLA_DV * (hd + 1)]
            att = jnp.where(eye, _dot_nt(q[:, ks], k[:, ks]), 0.0)
            for lv in range(n_lvl):
                att = att + lm_ref[lv] * _dot_nt(aq[lv][:, ks], ak[lv][:, ks])
            st = st_s[bi, hd]
            outs.append(_dot_nt(q_in[:, ks], st) + _dot(att, vh))
            st_s[bi, hd] = st * decay_end[:, ks] + _dot_tn(vh, k_end[:, ks])
        br_ref[bi] = _gla_out(outs, gout_ref, h[:, 1152:1664])

    @pl.when(ci == pl.num_programs(0) - 1)
    def _():
        for bi in range(h_ref.shape[0]):
            for hd in range(GLA_HEADS):
                state_ref[bi, hd] = st_s[bi, hd].T


def _gla_weights(p):
    wa = jnp.zeros((128, GLA_HEADS * GLA_DK), F32).at[:GLA_RANK].set(p["gla_w_a2"]).astype(BF16)
    return wa, p["gla_b_a"].reshape(1, -1), p["gla_out_g"].reshape(1, -1)


def _gla_prompt(h, p):
    b, t, _ = h.shape
    c = GLA_CHUNK
    cum, lm = _gla_constants(c)
    assert b <= 4
    const2 = lambda r, w: pl.BlockSpec((r, w), lambda i: (0, 0))
    return pl.pallas_call(
        _gla_kernel, grid=(t // c,),
        in_specs=[pl.BlockSpec((b, c, SEG_GLA), lambda i: (0, i, 0)),
                  const2(128, 256), const2(1, 256), const2(1, GLA_DV), const2(cum.shape[0], c),
                  pl.BlockSpec(lm.shape, lambda i: (0, 0, 0))],
        out_specs=[pl.BlockSpec((b, c, 512), lambda i: (0, i, 0)),
                   pl.BlockSpec((b, GLA_HEADS, GLA_DK, GLA_DV), lambda i: (0, 0, 0, 0))],
        out_shape=[jax.ShapeDtypeStruct((b, t, 512), BF16),
                   jax.ShapeDtypeStruct((b, GLA_HEADS, GLA_DK, GLA_DV), F32)],
        scratch_shapes=[pltpu.VMEM((b, GLA_HEADS, GLA_DV, GLA_DK), F32)],
        compiler_params=_cparams("arbitrary"), name="gla_prompt")(h, *_gla_weights(p), cum, lm)


def _gla_sample_kernel(h_ref, s0_ref, wa_ref, ba_ref, gout_ref, br_ref, snew_ref):
    eye = (lax.broadcasted_iota(jnp.int32, (GLA_DK, GLA_DK), 0)
           == lax.broadcasted_iota(jnp.int32, (GLA_DK, GLA_DK), 1))
    to_col = lambda r: jnp.sum(jnp.where(eye, r, 0.0), axis=1, keepdims=True)
    for i in range(h_ref.shape[0]):
        h = h_ref[i]
        q = h[:, 0:256] * (GLA_DK ** -0.5)
        k = h[:, 256:512]
        v = h[:, 512:1024]
        g = _gla_log_alpha(h[:, 1024:1152], wa_ref, ba_ref)
        decay = jnp.exp(g)
        outs = []
        for hd in range(GLA_HEADS):
            ks = slice(GLA_DK * hd, GLA_DK * (hd + 1))
            vh = v[:, GLA_DV * hd:GLA_DV * (hd + 1)]
            s0 = s0_ref[0, i, hd]
            qk = jnp.sum(q[:, ks] * k[:, ks], axis=-1, keepdims=True)
            outs.append(_dot(q[:, ks] * decay[:, ks], s0) + qk * vh)
            snew_ref[i, hd] = to_col(decay[:, ks]) * s0 + to_col(k[:, ks]) * vh
        br_ref[i] = _gla_out(outs, gout_ref, h[:, 1152:1664])


def _gla_sample(h, state, layer, p):
    s = h.shape[0]
    per = GLA_SAMPLE_SEQS if s % GLA_SAMPLE_SEQS == 0 else 1
    const2 = lambda r, w: pl.BlockSpec((r, w), lambda b: (0, 0))
    br, snew = pl.pallas_call(
        _gla_sample_kernel, grid=(s // per,),
        in_specs=[pl.BlockSpec((per, 1, SEG_GLA), lambda b: (b, 0, 0)),
                  pl.BlockSpec((1, per, GLA_HEADS, GLA_DK, GLA_DV), lambda b: (layer, b, 0, 0, 0)),
                  const2(128, 256), const2(1, 256), const2(1, GLA_DV)],
        out_specs=[pl.BlockSpec((per, 1, 512), lambda b: (b, 0, 0)),
                   pl.BlockSpec((per, GLA_HEADS, GLA_DK, GLA_DV), lambda b: (b, 0, 0, 0))],
        out_shape=[jax.ShapeDtypeStruct((s, 1, 512), BF16),
                   jax.ShapeDtypeStruct((s, GLA_HEADS, GLA_DK, GLA_DV), F32)],
        compiler_params=_cparams("parallel"), name="gla_sample",
    )(h.reshape(s, 1, SEG_GLA), state, *_gla_weights(p))
    return br.reshape(s, 512), snew


def _mla_prep_kernel(h_ref, cq_g_ref, wuq_ref, grp_ref, qg_ref, wuk_ref, ckv_g_ref, kr_g_ref,
                     cosq_ref, sinq_ref, cosk_ref, sink_ref, qcat_ref, lat_ref):
    h = h_ref[0]
    cq = h[:, 512:896]
    cqn =cq * lax.rsqrt(jnp.mean(cq * cq, axis=-1, keepdims=True) + NORM_EPS) * cq_g_ref[...]
    q = _dot(cqn, wuq_ref[...])
    ms = _dot_x3(q * q, grp_ref[...]) * (1.0 / (MLA_NOPE + MLA_ROPE))
    qn = q * lax.rsqrt(ms + NORM_EPS) * qg_ref[...]
    q_rope = _rope64(qn[:, 512:768], cosq_ref[...], sinq_ref[...])
    for hd in range(MLA_HEADS):
        q_lat = _dot(qn[:, MLA_NOPE * hd:MLA_NOPE * (hd + 1)], wuk_ref[hd])
        qcat_ref[0, hd] = jnp.concatenate(
            [q_lat.T, q_rope[:, MLA_ROPE * hd:MLA_ROPE * (hd + 1)].T], axis=0).astype(BF16)
    ckv = h[:, 896:1024]
    c =ckv * lax.rsqrt(jnp.mean(ckv * ckv, axis=-1, keepdims=True) + NORM_EPS) * ckv_g_ref[...]
    kr = h[:, 1024:1152]
    ms_kr = jnp.sum(kr * kr, axis=-1, keepdims=True) * (1.0 / MLA_ROPE)
    krn = kr * lax.rsqrt(ms_kr + NORM_EPS) * kr_g_ref[...]
    krr = _rope64(krn, cosk_ref[...], sink_ref[...])
    lat_ref[0] = jnp.concatenate([c, krr[:, 0:MLA_ROPE]], axis=1)


def _mla_weights(p):
    hw = MLA_NOPE + MLA_ROPE
    w = p["mla_w_uq"].reshape(MLA_Q_LORA, MLA_HEADS, hw)
    wuq = jnp.concatenate([w[:, :, :MLA_NOPE].reshape(MLA_Q_LORA, -1),
                           w[:, :, MLA_NOPE:].reshape(MLA_Q_LORA, -1)], axis=1).astype(BF16)
    qg = jnp.concatenate([jnp.tile(p["mla_q_g"][:MLA_NOPE], MLA_HEADS),
                          jnp.tile(p["mla_q_g"][MLA_NOPE:], MLA_HEADS)]).reshape(1, -1)
    head = np.concatenate([np.arange(MLA_HEADS * MLA_NOPE) // MLA_NOPE, np.arange(MLA_HEADS * MLA_ROPE) // MLA_ROPE])
    grp = jnp.asarray((head[:, None] == head[None, :]).astype(np.float32), dtype=BF16)
    wuk_t = jnp.swapaxes(p["mla_w_uk"], 1, 2).astype(BF16)
    kr_g = jnp.concatenate([p["mla_kr_g"], jnp.zeros((128 - MLA_ROPE,), F32)]).reshape(1, 128)
    return (p["mla_cq_g"].reshape(1, -1), wuq, grp, qg, wuk_t, p["mla_ckv_g"].reshape(1, -1), kr_g)


def _mla_prep(h, tabs, p, tq):
    b, t, _ = h.shape
    tab = lambda w: pl.BlockSpec((tq, w), lambda bi, i: (i, 0))
    const2 = lambda r, w: pl.BlockSpec((r, w), lambda bi, i: (0, 0))
    return pl.pallas_call(
        _mla_prep_kernel, grid=(b, t // tq),
        in_specs=[pl.BlockSpec((1, tq, SEG_MLA), lambda bi, i: (bi, i, 0)),
                  const2(1, 384), const2(384, 768), const2(768, 768), const2(1, 768),
                  pl.BlockSpec((MLA_HEADS, MLA_NOPE, MLA_KV_LORA), lambda bi, i: (0, 0, 0)),
                  const2(1, 128), const2(1, 128), tab(256), tab(256), tab(128), tab(128)],
        out_specs=[pl.BlockSpec((1, MLA_HEADS, MLA_ROW, tq), lambda bi, i: (bi, 0, 0, i)),
                   pl.BlockSpec((1, tq, MLA_ROW), lambda bi, i: (bi, i, 0))],
        out_shape=[jax.ShapeDtypeStruct((b, MLA_HEADS, MLA_ROW, t), BF16),
                   jax.ShapeDtypeStruct((b, t, MLA_ROW), F32)],
        compiler_params=_cparams("parallel", "parallel"), name="mla_prep")(h, *_mla_weights(p), *tabs)


def _mla_up(o_lat, wuv_ref, z):
    outs = [_dot(o_lat[hd], wuv_ref[hd]) for hd in range(MLA_HEADS)]
    return (jnp.concatenate(outs, axis=1) * _silu(z)).astype(BF16)


def _mla_attn_kernel(q_ref, lat_ref, wuv_ref, z_ref, out_ref, m_s, l_s, acc_s, *, tq):
    i = pl.program_id(1)
    q_t = jnp.concatenate([q_ref[0, hd] for hd in range(MLA_HEADS)], axis=1)
    stats = (m_s, l_s, acc_s)
    _flash_reset(*stats)

    def tile(j, bias):
        start = pl.multiple_of(j * tq, tq)
        lat = lat_ref[0, pl.ds(start, tq), :].astype(BF16)
        s_t = _dot(lat, q_t) * MLA_SCALE
        if bias is not None:
            s_t = s_t + jnp.concatenate([bias] * MLA_HEADS, axis=1)
        _flash_update(s_t, lat[:, 0:MLA_KV_LORA], *stats)

    def full_tile(j, carry):
        tile(j, None)
        return carry

    lax.fori_loop(0, i, full_tile, 0)
    causal = (lax.broadcasted_iota(jnp.int32, (tq, tq), 0) <= lax.broadcasted_iota(jnp.int32, (tq, tq), 1))
    tile(i, jnp.where(causal, 0.0, NEG))
    o_t = acc_s[...] / l_s[...]
    out_ref[0] = _mla_up([o_t[:, tq * hd:tq * (hd + 1)].T for hd in range(MLA_HEADS)], wuv_ref, z_ref[0])


def _mla_attn(qcat_t, lat, w_uv, h, tq):
    b, _, _, t = qcat_t.shape
    cols = MLA_HEADS * tq
    return pl.pallas_call(
        functools.partial(_mla_attn_kernel, tq=tq), grid=(b, t // tq),
        in_specs=[pl.BlockSpec((1, MLA_HEADS, MLA_ROW, tq), lambda bi, i: (bi, 0, 0, i)),
                  pl.BlockSpec((1, t, MLA_ROW), lambda bi, i: (bi, 0, 0)),
                  pl.BlockSpec((MLA_HEADS, MLA_KV_LORA, MLA_VHD), lambda bi, i: (0, 0, 0)),
                  pl.BlockSpec((1, tq, 512), lambda bi, i: (bi, i, 0))],
        out_specs=pl.BlockSpec((1, tq, 512), lambda bi, i: (bi, i, 0)),
        out_shape=jax.ShapeDtypeStruct((b, t, 512), BF16),
        scratch_shapes=[pltpu.VMEM((1, cols), F32), pltpu.VMEM((1, cols), F32),
                        pltpu.VMEM((MLA_KV_LORA, cols), F32)],
        compiler_params=_cparams("parallel", "arbitrary"), name="mla_attn")(qcat_t, lat, w_uv.astype(BF16), h)


def _mla_sample_kernel(pt_ref, cache_ref, q_ref, new_ref, o_ref, buf, sem, m_s, l_s, acc_s, *, layer, n_pages):
    group = buf.shape[1]
    n_groups = n_pages // group

    def compute(b, g, slot):
        q = q_ref[b]
        first = g == 0
        m_run = jnp.where(first, NEG, m_s[...])
        l_run = jnp.where(first, 0.0, l_s[...])
        acc_run = jnp.where(first, 0.0, acc_s[...])
        pages = [buf[slot, k].astype(BF16) for k in range(group)]
        scores = [_dot(q, pg) * MLA_SCALE for pg in pages]
        m_pg = [jnp.max(s, axis=-1, keepdims=True) for s in scores]
        probs = [jnp.exp(s - mk) for s, mk in zip(scores, m_pg)]
        l_pg = [jnp.sum(p, axis=-1, keepdims=True) for p in probs]
        o_pg = [_dot_nt(p, pg[0:MLA_KV_LORA, :]) for p, pg in zip(probs, pages)]
        m_new = functools.reduce(jnp.maximum, m_pg, m_run)
        w_run = jnp.exp(m_run - m_new)
        l_new = w_run * l_run
        acc = w_run * acc_run
        for mk, lk, ok in zip(m_pg, l_pg, o_pg):
            w = jnp.exp(mk - m_new)
            l_new = l_new + w * lk
            acc = acc + w * ok
        m_s[...] = m_new
        l_s[...] = l_new
        acc_s[...] = acc

        @pl.when(g == n_groups - 1)
        def _():
            new = new_ref[b]
            s_new = jnp.sum(q.astype(F32) * new, axis=-1, keepdims=True) * MLA_SCALE
            m_fin = jnp.maximum(m_new, s_new)
            a_fin = jnp.exp(m_new - m_fin)
            p_new = jnp.exp(s_new - m_fin)
            o_ref[b] = (a_fin * acc + p_new * new[:, 0:MLA_KV_LORA]) / (a_fin * l_new + p_new)

    m_s[...] = jnp.full_like(m_s, NEG)
    l_s[...] = jnp.zeros_like(l_s)
    acc_s[...] = jnp.zeros_like(acc_s)
    _page_stream(pt_ref, buf, sem, n_groups, group, lambda page: cache_ref.at[layer, page], compute)


def _mla_sample_attn(cache_mla_t, layer, page_table, q, new_rows, pages_per_step):
    s, n_pages = page_table.shape
    group = pages_per_step
    assert n_pages % group == 0
    vmem = pl.BlockSpec(memory_space=pltpu.VMEM)
    return pl.pallas_call(
        functools.partial(_mla_sample_kernel, layer=layer, n_pages=n_pages),
        in_specs=[pl.BlockSpec(memory_space=pltpu.SMEM), pl.BlockSpec(memory_space=pl.ANY), vmem, vmem],
        out_specs=vmem,
        out_shape=jax.ShapeDtypeStruct((s, MLA_HEADS, MLA_KV_LORA), F32),
        scratch_shapes=[pltpu.VMEM((MLA_STREAM_SLOTS, group, MLA_ROW, PAGE_SIZE), F32),
                        pltpu.SemaphoreType.DMA((MLA_STREAM_SLOTS,)),
                        pltpu.VMEM((MLA_HEADS, 1), F32), pltpu.VMEM((MLA_HEADS, 1), F32),
                        pltpu.VMEM((MLA_HEADS, MLA_KV_LORA), F32)],
        compiler_params=pltpu.CompilerParams(vmem_limit_bytes=VMEM_LIMIT), name="mla_sample_attn",
    )(page_table, cache_mla_t, q, new_rows)


def _mla_post_sample_kernel(o_ref, wuv_ref, z_ref, out_ref):
    out_ref[...] = _mla_up([o_ref[:, hd, :] for hd in range(MLA_HEADS)], wuv_ref, z_ref[...])


def _mla_post_sample(o_lat, w_uv, z):
    s = o_lat.shape[0]
    return pl.pallas_call(
        _mla_post_sample_kernel, out_shape=jax.ShapeDtypeStruct((s, 512), BF16), name="mla_post_sample",
    )(o_lat, w_uv.astype(BF16), z)


def _pad_cols(w, width):
    return jnp.pad(w, ((0, 0), (0, width - w.shape[1])))


def _input_weights(w_in):
    sizes = (512, 384, 24, 512, 512, 512, 256, 256, 512, 16, 512, 384, 128, 64, 512)
    offs = np.concatenate([[0], np.cumsum(sizes)])
    seg = [w_in[:, offs[i]:offs[i + 1]] for i in range(len(sizes))]
    (nsa_q, nsa_kv, nsa_g, nsa_z, lru_x, lru_z, gla_q, gla_k, gla_v, gla_a, gla_z,
     mla_cq, mla_ckv, mla_kr, mla_z) = seg
    w_nsa = jnp.concatenate([nsa_q, nsa_kv, _pad_cols(nsa_g, 128), nsa_z], axis=1)
    w_lru = jnp.concatenate([lru_x, lru_z], axis=1)
    w_gla = jnp.concatenate([gla_q, gla_k, gla_v, _pad_cols(gla_a, 128), gla_z], axis=1)
    w_mla = jnp.concatenate([mla_z, mla_cq, mla_ckv, _pad_cols(mla_kr, 128)], axis=1)
    w_merge = w_in[:, offs[-1]:]
    return tuple(w.astype(BF16) for w in (w_nsa, w_lru, w_gla, w_mla, w_merge))


def _layer_params(params, layer):
    return {k: v[layer] for k, v in params.items()}


def _prompt_layer(x, p, tabs):
    b, t, d = x.shape
    n = b * t
    w_nsa, w_lru, w_gla, w_mla, w_merge = _input_weights(p["w_in"])
    xn = _rmsnorm_bf16(x.reshape(n, d), p["norm_g"], 512)
    seg = lambda w, name: _matmul(xn, w, 512, name).reshape(b, t, -1)
    h_nsa, h_lru, h_gla, h_mla = (seg(w_nsa, "in_nsa"), seg(w_lru, "in_lru"), seg(w_gla, "in_gla"),
                                  seg(w_mla, "in_mla"))
    nsa_tabs, mla_tabs = tabs
    tq = 128
    qn, qr, rows, win, gate = _nsa_prep(h_nsa, nsa_tabs, p["nsa_q_g"], p["nsa_k_g"], tq)
    comp = _compress(rows, p["nsa_cmp_w"])
    o_cmp, sel = _nsa_cmp(qn, comp, tq)
    br_nsa = _nsa_attn(qr, sel, rows, win, o_cmp, gate, h_nsa, tq)
    br_lru, h_last, conv_new = _lru_prompt(h_lru, p, 256)
    br_gla, gla_state = _gla_prompt(h_gla, p)
    qcat, lat = _mla_prep(h_mla, mla_tabs, p, tq)
    br_mla = _mla_attn(qcat, lat, p["mla_w_uv"], h_mla, 256)
    flat = lambda a: a.reshape(n, BRANCH_W)
    merged = _merge(xn, [flat(br_nsa), flat(br_lru), flat(br_gla), flat(br_mla)], w_merge,
                    p["w_branch"].astype(BF16), MERGE_ROWS, MERGE_COLS)
    y = _outproj(merged, p["w_out"].astype(BF16), x.reshape(n, d), 256).reshape(b, t, d)
    keep = min(WINDOW, t)
    states = (rows.reshape(b, t, 4, NSA_HD), win[:, t - keep:].reshape(b, keep, 2, NSA_HD),
              h_last.reshape(b, BRANCH_W), conv_new, gla_state, lat)
    return y, states


def _sample_layer(x, p, layer, caches, tabs):
    s, t, d = x.shape
    cache_nsa_kv, cache_nsa_win, state_lru_h, state_lru_conv, state_gla, cache_mla, page_table = caches
    w_nsa, w_lru, w_gla, w_mla, w_merge = _input_weights(p["w_in"])
    x2 = x.reshape(s, d)
    xn = _rmsnorm_bf16(x2, p["norm_g"], s)
    h_nsa = _matmul(xn, w_nsa, s, "in_nsa_s")
    h_lru = _matmul(xn, w_lru, s, "in_lru_s")
    h_gla = _matmul(xn, w_gla, s, "in_gla_s")
    h_mla = _matmul(xn, w_mla, s, "in_mla_s")
    nsa_tabs, mla_tabs = tabs
    qn, qr, rows, win, gate = _nsa_prep(h_nsa[None], nsa_tabs, p["nsa_q_g"], p["nsa_k_g"], s)
    heads = lambda a: a.reshape(s, NSA_HEADS, NSA_HD)
    n_pages = page_table.shape[1]
    o_cmp, sel = _nsa_cmp_sample(cache_nsa_kv, layer, page_table, heads(qn[0]), p["nsa_cmp_w"],
                                 min(CMP_STREAM_PAGES, n_pages))
    kvr = jnp.concatenate([rows[0], win[0]], axis=1).reshape(s, 1, 384)
    o_slc, o_win, new_win_t = _nsa_slc_sample(cache_nsa_kv, cache_nsa_win, layer, page_table,
                                              sel.reshape(s, 128), heads(qr[0]), kvr)
    flat = lambda a: a.reshape(s, BRANCH_W)
    br_nsa = _nsa_post_sample(flat(o_cmp), flat(o_slc), flat(o_win), gate[0], h_nsa[:, 1024:1536])
    br_lru, h_new = _lru_sample(h_lru, state_lru_conv[layer], state_lru_h[layer], p)
    conv_new = jnp.swapaxes(jnp.stack([state_lru_conv[layer, 1], state_lru_conv[layer, 2], h_lru[:, 0:512]]), 0, 1)
    br_gla, gla_state = _gla_sample(h_gla, state_gla, layer, p)
    qcat_t, lat = _mla_prep(h_mla[None], mla_tabs, p, s)
    o_lat = _mla_sample_attn(cache_mla, layer, page_table, jnp.transpose(qcat_t[0], (2, 0, 1)),
                             lat[0].reshape(s, 1, MLA_ROW), min(MLA_STREAM_PAGES, n_pages))
    br_mla = _mla_post_sample(o_lat, p["mla_w_uv"], h_mla[:, 0:512])
    merged = _merge(xn, [br_nsa, br_lru, br_gla, br_mla], w_merge, p["w_branch"].astype(BF16), s, MERGE_COLS)
    y = _outproj(merged, p["w_out"].astype(BF16), x2, s).reshape(s, t, d)
    states = (rows[0].reshape(s, 1, 4, NSA_HD), jnp.transpose(new_win_t[0], (0, 3, 1, 2)), h_new, conv_new,
              gla_state, lat[0].reshape(s, 1, MLA_ROW))
    return y, states


def kernel(x_prompt, x_sample, cache_nsa_kv, cache_nsa_win, state_lru_h, state_lru_conv, state_gla, cache_mla,
           page_table, norm_g, w_in, nsa_q_g, nsa_k_g, nsa_cmp_w, lru_conv_w, lru_conv_b, lru_wa, lru_ba, lru_wx,
           lru_bx, lru_lam, gla_w_a2, gla_b_a, gla_out_g, mla_cq_g, mla_w_uq, mla_q_g, mla_ckv_g, mla_kr_g,
           mla_w_uk, mla_w_uv, w_branch, w_out):
    params = dict(norm_g=norm_g, w_in=w_in, nsa_q_g=nsa_q_g, nsa_k_g=nsa_k_g, nsa_cmp_w=nsa_cmp_w,
                  lru_conv_w=lru_conv_w, lru_conv_b=lru_conv_b, lru_wa=lru_wa, lru_ba=lru_ba, lru_wx=lru_wx,
                  lru_bx=lru_bx, lru_lam=lru_lam, gla_w_a2=gla_w_a2, gla_b_a=gla_b_a, gla_out_g=gla_out_g,
                  mla_cq_g=mla_cq_g, mla_w_uq=mla_w_uq, mla_q_g=mla_q_g, mla_ckv_g=mla_ckv_g,
                  mla_kr_g=mla_kr_g, mla_w_uk=mla_w_uk, mla_w_uv=mla_w_uv, w_branch=w_branch, w_out=w_out)
    depth = w_in.shape[0]
    assert x_prompt.shape[2] == N_BRANCH * BRANCH_W and x_sample.shape[1] == 1
    t = x_prompt.shape[1]
    n_seq, n_pages = page_table.shape
    past_len = n_pages * PAGE_SIZE
    assert t % 256 == 0 and past_len % SLC_BLOCK == 0 and cache_nsa_win.shape[2] == WINDOW
    caches = (jnp.transpose(cache_nsa_kv, (0, 1, 3, 4, 2)), jnp.transpose(cache_nsa_win, (0, 1, 3, 4, 2)),
              state_lru_h, jnp.transpose(state_lru_conv, (0, 2, 1, 3)), state_gla,
              jnp.transpose(cache_mla, (0, 1, 3, 2)), page_table)
    tabs_p = _rope_tables(jnp.arange(t))
    tabs_s = _rope_tables(jnp.full((n_seq,), past_len))
    y_p, y_s = x_prompt, x_sample
    sp, ss = [], []
    for layer in range(depth):
        p = _layer_params(params, layer)
        y_p, st = _prompt_layer(y_p, p, tabs_p)
        sp.append(st)
        y_s, st = _sample_layer(y_s, p, layer, caches, tabs_s)
        ss.append(st)
    stack = lambda lst, i: jnp.stack([s[i] for s in lst])
    return (y_p, y_s, stack(sp, 0), stack(ss, 0), stack(sp, 1), stack(ss, 1), stack(sp, 2), stack(ss, 2),
            stack(sp, 3), stack(ss, 3), stack(sp, 4), stack(ss, 4), stack(sp, 5), stack(ss, 5))
```
